```python
import jax, jax.numpy as jnp
from jax import lax
import numpy as np

D_MODEL = 1024
BATCH = 4
SEQ = 8192
DEPTH = 2

ATTN_HEADS = 8
ATTN_HEAD_DIM = 128
ATTN_WIDTH = ATTN_HEADS * ATTN_HEAD_DIM
MOBA_BLOCK = 256
MOBA_TOPK = 3
MOBA_QUERY_BLOCK = 32
SSM_HEADS = 16
SSM_HEAD_DIM = 64
SSM_WIDTH = SSM_HEADS * SSM_HEAD_DIM
SSM_GROUPS = 2
SSM_STATE = 128
SSM_CONV = 4
SSM_CHUNK = 128
SSM_CONV_CH = SSM_WIDTH + 2 * SSM_GROUPS * SSM_STATE
SGU_GROUPS = 8
SGU_GROUP_DIM = 128
SGU_WIDTH = SGU_GROUPS * SGU_GROUP_DIM
SGU_CHUNK = 128
N_BRANCHES = 3
FFN_HIDDEN = 2816
NORM_EPS = 1e-6
IN_SIZES = (ATTN_WIDTH, ATTN_WIDTH, ATTN_WIDTH, SSM_WIDTH, SSM_CONV_CH, SSM_HEADS,
            SGU_WIDTH, SGU_WIDTH, N_BRANCHES * D_MODEL)
IN_COLS = sum(IN_SIZES)

kernel_name = "hybrid_moba_ssd_gmlp_macaron_block"

F32 = jnp.float32


def rms_norm(x, g):
    xf = x.astype(F32)
    y = xf * lax.rsqrt(jnp.mean(xf * xf, axis=-1, keepdims=True) + NORM_EPS)
    return (y * g.astype(F32)).astype(x.dtype)


def layer_norm(x, g, b):
    xf = x.astype(F32)
    mu = jnp.mean(xf, axis=-1, keepdims=True)
    var = jnp.mean(jnp.square(xf - mu), axis=-1, keepdims=True)
    y = (xf - mu) * lax.rsqrt(var + NORM_EPS)
    return (y * g.astype(F32) + b.astype(F32)).astype(x.dtype)


def swiglu_ffn(x, w_gu, w_down):
    gate, up = jnp.split(x @ w_gu, 2, axis=-1)
    return (jax.nn.silu(gate) * up) @ w_down


def alibi_slopes(n_heads):
    return 2.0 ** (-8.0 * (jnp.arange(n_heads, dtype=F32) + 1.0) / n_heads)


def moba_attention(q, k, v):
    b_, s, h, dh = q.shape
    nb = -(-s // MOBA_BLOCK)
    sp = nb * MOBA_BLOCK
    top = min(MOBA_TOPK, nb)
    qt = (q * (dh ** -0.5)).transpose(0, 2, 1, 3)
    pad = ((0, 0), (0, 0), (0, sp - s), (0, 0))
    kt = jnp.pad(k.transpose(0, 2, 1, 3), pad)
    vt = jnp.pad(v.transpose(0, 2, 1, 3), pad)
    k_blocks = kt.reshape(b_, h, nb, MOBA_BLOCK, dh)
    v_blocks = vt.reshape(b_, h, nb, MOBA_BLOCK, dh)
    k_mean = jnp.mean(k_blocks.astype(F32), axis=3)
    slopes = alibi_slopes(h)
    b_ix = jnp.arange(b_)[:, None, None, None]
    h_ix = jnp.arange(h)[None, :, None, None]
    blk_ids = jnp.arange(nb)
    offs = jnp.arange(MOBA_BLOCK)
    qb_len = MOBA_QUERY_BLOCK

    def one_query_block(start):
        qb = lax.dynamic_slice_in_dim(qt, start, qb_len, axis=2)
        t = start + jnp.arange(qb_len)
        own = start // MOBA_BLOCK
        gate = jnp.einsum('bhqd,bhnd->bhqn', qb.astype(F32), k_mean)
        gate = jnp.where(blk_ids < own, gate, -jnp.inf)
        _, sel = lax.top_k(gate, top)
        sel_valid = jnp.arange(top) < own
        k_sel = k_blocks[b_ix, h_ix, sel]
        v_sel = v_blocks[b_ix, h_ix, sel]
        s_sel = jnp.einsum('bhqd,bhqnkd->bhqnk', qb, k_sel).astype(F32)
        dist_sel = (t[:, None, None] - (sel[..., None] * MOBA_BLOCK + offs)).astype(F32)
        s_sel = s_sel - slopes[:, None, None, None] * dist_sel
        s_sel = jnp.where(sel_valid[:, None], s_sel, -jnp.inf)
        k_own = lax.dynamic_slice_in_dim(kt, own * MOBA_BLOCK, MOBA_BLOCK, axis=2)
        v_own = lax.dynamic_slice_in_dim(vt, own * MOBA_BLOCK, MOBA_BLOCK, axis=2)
        dist_own = t[:, None] - (own * MOBA_BLOCK + offs)[None, :]
        s_own = jnp.einsum('bhqd,bhkd->bhqk', qb, k_own).astype(F32)
        s_own = jnp.where(dist_own >= 0,
                          s_own - slopes[:, None, None] * dist_own.astype(F32), -jnp.inf)
        scores = jnp.concatenate(
            [s_own, s_sel.reshape(b_, h, qb_len, top * MOBA_BLOCK)], axis=-1)
        p = jax.nn.softmax(scores, axis=-1).astype(vt.dtype)
        p_own = p[..., :MOBA_BLOCK]
        p_sel = p[..., MOBA_BLOCK:].reshape(b_, h, qb_len, top, MOBA_BLOCK)
        return (jnp.einsum('bhqk,bhkd->bhqd', p_own, v_own)
                + jnp.einsum('bhqnk,bhqnkd->bhqd', p_sel, v_sel))

    starts = jnp.arange(0, s, qb_len)
    out = lax.map(one_query_block, starts)
    return out.transpose(1, 0, 3, 2, 4).reshape(b_, s, h * dh)


def causal_depthwise_conv(x, w, b):
    k = w.shape[0]
    y = lax.conv_general_dilated(x, w[:, None, :], window_strides=(1,),
                                 padding=[(k - 1, 0)],
                                 dimension_numbers=('NWC', 'WIO', 'NWC'),
                                 feature_group_count=x.shape[-1])
    return y + b


def ssd_scan(x, dt, a, bm, cm):
    b_, s, h, p = x.shape
    g, n = bm.shape[2], bm.shape[3]
    hg = h // g
    q = SSM_CHUNK
    nc = s // q
    xc = x.astype(F32).reshape(b_, nc, q, g, hg, p)
    dtc = dt.astype(F32).reshape(b_, nc, q, g, hg)
    bc = bm.astype(F32).reshape(b_, nc, q, g, n)
    cc = cm.astype(F32).reshape(b_, nc, q, g, n)
    a_cs = jnp.cumsum(dtc * a.astype(F32).reshape(g, hg), axis=2)
    tri = jnp.tril(jnp.ones((q, q), dtype=bool))[:, :, None, None]
    seg = a_cs[:, :, :, None] - a_cs[:, :, None, :]
    decay = jnp.exp(jnp.where(tri, seg, -jnp.inf))
    cb = jnp.einsum('bcign,bcjgn->bcijg', cc, bc)
    wts = cb[..., None] * decay * dtc[:, :, None]
    y_diag = jnp.einsum('bcijgh,bcjghp->bcighp', wts, xc)
    decay_to_end = jnp.exp(a_cs[:, :, -1:] - a_cs)
    states = jnp.einsum('bcjgn,bcjgh,bcjghp->bcghpn', bc, decay_to_end * dtc, xc)
    chunk_decay = jnp.exp(a_cs[:, :, -1])

    def step(carry, inp):
        st, dec = inp
        return carry * dec[..., None, None] + st, carry

    h0 = jnp.zeros((b_, g, hg, p, n), F32)
    _, h_prev = lax.scan(step, h0, (jnp.swapaxes(states, 0, 1), jnp.swapaxes(chunk_decay, 0, 1)))
    h_prev = jnp.swapaxes(h_prev, 0, 1)
    y_off = jnp.einsum('bcign,bcghpn,bcigh->bcighp', cc, h_prev, jnp.exp(a_cs))
    return (y_diag + y_off).reshape(b_, s, h, p).astype(x.dtype)


def hybrid_mixer(xn, w_in, conv_w, conv_b, dt_bias, a_log, d_skip, ssm_norm_g,
                 sgu_ln_g, sgu_ln_b, sgu_w, sgu_b, p_attn, p_ssm, p_sgu, w_out):
    b_, s, _ = xn.shape
    split_at = np.cumsum(IN_SIZES)[:-1].tolist()
    q, k, v, z, xbc, dt_raw, u, vg, gate_raw = jnp.split(xn @ w_in, split_at, axis=-1)

    y_a = moba_attention(q.reshape(b_, s, ATTN_HEADS, ATTN_HEAD_DIM),
                         k.reshape(b_, s, ATTN_HEADS, ATTN_HEAD_DIM),
                         v.reshape(b_, s, ATTN_HEADS, ATTN_HEAD_DIM))

    xbc = jax.nn.silu(causal_depthwise_conv(xbc, conv_w, conv_b))
    xs, bm, cm = jnp.split(xbc, [SSM_WIDTH, SSM_WIDTH + SSM_GROUPS * SSM_STATE], axis=-1)
    xs = xs.reshape(b_, s, SSM_HEADS, SSM_HEAD_DIM)
    bm = bm.reshape(b_, s, SSM_GROUPS, SSM_STATE)
    cm = cm.reshape(b_, s, SSM_GROUPS, SSM_STATE)
    dt = jax.nn.softplus(dt_raw.astype(F32) + dt_bias.astype(F32))
    a = -jnp.exp(a_log.astype(F32))
    y_s = ssd_scan(xs, dt, a, bm, cm) + d_skip[:, None] * xs
    y_s = rms_norm(y_s.reshape(b_, s, SSM_WIDTH) * jax.nn.silu(z), ssm_norm_g)

    u = jax.nn.gelu(u)
    vg = layer_norm(jax.nn.gelu(vg), sgu_ln_g, sgu_ln_b)
    vg = vg.reshape(b_, s // SGU_CHUNK, SGU_CHUNK, SGU_GROUPS, SGU_GROUP_DIM)
    w_sp = jnp.tril(sgu_w)
    sv = jnp.einsum('gij,bcjgd->bcigd', w_sp, vg) + sgu_b.T[None, None, :, :, None]
    y_c = u * sv.reshape(b_, s, SGU_WIDTH)

    gates = jax.nn.sigmoid(gate_raw).reshape(b_, s, N_BRANCHES, D_MODEL)
    merged = (gates[:, :, 0] * (y_a @ p_attn)
              + gates[:, :, 1] * (y_s @ p_ssm)
              + gates[:, :, 2] * (y_c @ p_sgu))
    return merged @ w_out


def setup_inputs(seed: int = 0) -> dict:
    key = jax.random.key(seed)
    ks = jax.random.split(key, 32)
    L = DEPTH

    def nrm(k, shape, scale):
        return jax.random.normal(k, shape, F32) * scale

    def gain(k, shape):
        return 1.0 + 0.05 * jax.random.normal(k, shape, F32)

    dt0 = jnp.exp(jax.random.uniform(ks[9], (L, SSM_HEADS), F32)
                  * (jnp.log(0.1) - jnp.log(0.001)) + jnp.log(0.001))
    return {
        "x": jax.random.normal(ks[0], (BATCH, SEQ, D_MODEL), F32),
        "ffn1_pre_g": gain(ks[1], (L, D_MODEL)),
        "ffn1_w_gu": nrm(ks[2], (L, D_MODEL, 2 * FFN_HIDDEN), D_MODEL ** -0.5),
        "ffn1_w_down": nrm(ks[3], (L, FFN_HIDDEN, D_MODEL), FFN_HIDDEN ** -0.5),
        "ffn1_post_g": gain(ks[4], (L, D_MODEL)),
        "mix_pre_g": gain(ks[5], (L, D_MODEL)),
        "w_in": nrm(ks[6], (L, D_MODEL, IN_COLS), D_MODEL ** -0.5),
        "conv_w": nrm(ks[7], (L, SSM_CONV, SSM_CONV_CH), SSM_CONV ** -0.5),
        "conv_b": nrm(ks[8], (L, SSM_CONV_CH), 0.02),
        "dt_bias": dt0 + jnp.log(-jnp.expm1(-dt0)),
        "a_log": jnp.log(jax.random.uniform(ks[10], (L, SSM_HEADS), F32, 1.0, 16.0)),
        "d_skip": gain(ks[11], (L, SSM_HEADS)),
        "ssm_norm_g": gain(ks[12], (L, SSM_WIDTH)),
        "sgu_ln_g": gain(ks[13], (L, SGU_WIDTH)),
        "sgu_ln_b": nrm(ks[14], (L, SGU_WIDTH), 0.02),
        "sgu_w": nrm(ks[15], (L, SGU_GROUPS, SGU_CHUNK, SGU_CHUNK), SGU_CHUNK ** -0.5),
        "sgu_b": gain(ks[16], (L, SGU_GROUPS, SGU_CHUNK)),
        "p_attn": nrm(ks[17], (L, ATTN_WIDTH, D_MODEL), ATTN_WIDTH ** -0.5),
        "p_ssm": nrm(ks[18], (L, SSM_WIDTH, D_MODEL), SSM_WIDTH ** -0.5),
        "p_sgu": nrm(ks[19], (L, SGU_WIDTH, D_MODEL), SGU_WIDTH ** -0.5),
        "w_out": nrm(ks[20], (L, D_MODEL, D_MODEL), D_MODEL ** -0.5),
        "mix_post_g": gain(ks[21], (L, D_MODEL)),
        "ffn2_pre_g": gain(ks[22], (L, D_MODEL)),
        "ffn2_w_gu": nrm(ks[23], (L, D_MODEL, 2 * FFN_HIDDEN), D_MODEL ** -0.5),
        "ffn2_w_down": nrm(ks[24], (L, FFN_HIDDEN, D_MODEL), FFN_HIDDEN ** -0.5),
        "ffn2_post_g": gain(ks[25], (L, D_MODEL)),
    }


def reference(x, ffn1_pre_g, ffn1_w_gu, ffn1_w_down, ffn1_post_g, mix_pre_g, w_in,
              conv_w, conv_b, dt_bias, a_log, d_skip, ssm_norm_g, sgu_ln_g, sgu_ln_b,
              sgu_w, sgu_b, p_attn, p_ssm, p_sgu, w_out, mix_post_g, ffn2_pre_g,
              ffn2_w_gu, ffn2_w_down, ffn2_post_g):
    h = x
    for i in range(DEPTH):
        f1 = swiglu_ffn(rms_norm(h, ffn1_pre_g[i]), ffn1_w_gu[i], ffn1_w_down[i])
        h = h + 0.5 * rms_norm(f1, ffn1_post_g[i])
        m = hybrid_mixer(rms_norm(h, mix_pre_g[i]), w_in[i], conv_w[i], conv_b[i],
                         dt_bias[i], a_log[i], d_skip[i], ssm_norm_g[i], sgu_ln_g[i],
                         sgu_ln_b[i], sgu_w[i], sgu_b[i], p_attn[i], p_ssm[i], p_sgu[i],
                         w_out[i])
        h = h + rms_norm(m, mix_post_g[i])
        f2 = swiglu_ffn(rms_norm(h, ffn2_pre_g[i]), ffn2_w_gu[i], ffn2_w_down[i])
        h = h + 0.5 * rms_norm(f2, ffn2_post_g[i])
    return h
```

```python
import functools

import jax
import jax.numpy as jnp
from jax import lax
from jax.experimental import pallas as pl
from jax.experimental.pallas import tpu as pltpu

F32 = jnp.float32
BF16 = jnp.bfloat16

NORM_EPS = 1e-6
ATTN_HEADS = 8
ATTN_HEAD_DIM = 128
MOBA_BLOCK = 256
MOBA_TOPK = 3
SSM_HEADS = 16
SSM_HEAD_DIM = 64
SSM_WIDTH = SSM_HEADS * SSM_HEAD_DIM
SSM_GROUPS = 2
SSM_STATE = 128
SSM_CONV = 4
SSM_CHUNK = 128
SSM_BC = SSM_GROUPS * SSM_STATE
SSM_CONV_CH = SSM_WIDTH + 2 * SSM_BC
SGU_GROUPS = 8
SGU_GROUP_DIM = 128
SGU_CHUNK = 128
N_BRANCHES = 3

LANES = 128
SUBLANES = 8
VMEM_LIMIT = 56 * 1024 * 1024
NEG_BIG = -1e30
POS_BIG = 1e30

FFN_TOKENS = 512
FFN_HIDDEN_CHUNK = 256
INPROJ_TOKENS = 256
INPROJ_COL_CHUNK = 512
SSM_TOKENS = 512
MERGE_TOKENS = 256


def _resident(shape):
    nd = len(shape)
    return pl.BlockSpec(shape, lambda *_: (0,) * nd, pipeline_mode=pl.Buffered(1))


def _rms(x, g):
    return x * lax.rsqrt(jnp.mean(x * x, axis=-1, keepdims=True) + NORM_EPS) * g


def _sigmoid(x):
    return 1.0 / (1.0 + jnp.exp(-x))


def _gelu_tanh(x):
    return x * (0.5 * (1.0 + jnp.tanh(0.7978845608028654 * (x + 0.044715 * (x * x * x)))))


def _dot(a, b):
    return jnp.dot(a, b, preferred_element_type=F32)


def _dot_nt(a, b):
    return lax.dot_general(a, b, (((1,), (1,)), ((), ())), preferred_element_type=F32)


def _dot_tn(a, b):
    return lax.dot_general(a, b, (((0,), (0,)), ((), ())), preferred_element_type=F32)


def _split2(x):
    hi = x.astype(BF16)
    lo = (x - hi.astype(F32)).astype(BF16)
    return hi, lo


def _split3(x):
    hi = x.astype(BF16)
    r = x - hi.astype(F32)
    mid = r.astype(BF16)
    lo = (r - mid.astype(F32)).astype(BF16)
    return hi, mid, lo


def _ffn_kernel(h_ref, pre_ref, wgu_ref, wd_ref, post_ref, o_ref, acc_ref):
    x = h_ref[...]
    xn = _rms(x, pre_ref[...]).astype(BF16)
    hidden = wd_ref.shape[0]
    fc = FFN_HIDDEN_CHUNK
    for c in range(hidden // fc):
        g = _dot(xn, wgu_ref[:, c * fc:(c + 1) * fc])
        u = _dot(xn, wgu_ref[:, hidden + c * fc:hidden + (c + 1) * fc])
        a = (g * _sigmoid(g) * u).astype(BF16)
        d = _dot(a, wd_ref[c * fc:(c + 1) * fc, :])
        if c == 0:
            acc_ref[...] = d
        else:
            acc_ref[...] += d
    o_ref[...] = x + 0.5 * _rms(acc_ref[...], post_ref[...])


def _ffn(h, pre_g, w_gu, w_down, post_g):
    t, d = h.shape
    hidden = w_down.shape[0]
    assert hidden % FFN_HIDDEN_CHUNK == 0 and t % FFN_TOKENS == 0
    tm = FFN_TOKENS
    return pl.pallas_call(
        _ffn_kernel,
        out_shape=jax.ShapeDtypeStruct((t, d), F32),
        grid=(t // tm,),
        in_specs=[
            pl.BlockSpec((tm, d), lambda i: (i, 0)),
            _resident((1, d)),
            _resident((d, 2 * hidden)),
            _resident((hidden, d)),
            _resident((1, d)),
        ],
        out_specs=pl.BlockSpec((tm, d), lambda i: (i, 0)),
        scratch_shapes=[pltpu.VMEM((tm, d), F32)],
        compiler_params=pltpu.CompilerParams(
            dimension_semantics=("arbitrary",), vmem_limit_bytes=VMEM_LIMIT),
        name="ffn",
    )(h, pre_g.reshape(1, d), w_gu, w_down, post_g.reshape(1, d))


def _inproj_layout(d_model):
    aw = ATTN_HEADS * ATTN_HEAD_DIM
    sw = SGU_GROUPS * SGU_GROUP_DIM
    return (("q", aw, aw), ("k", aw, aw), ("v", aw, aw), ("z", SSM_WIDTH, SSM_WIDTH),
            ("xbc", SSM_CONV_CH, SSM_CONV_CH), ("dt", SSM_HEADS, LANES),
            ("u", sw, sw), ("vg", sw, sw), ("gates", N_BRANCHES * d_model, N_BRANCHES * d_model))


def _inproj_kernel(h_ref, pre_ref, w_ref, q_ref, k_ref, v_ref, z_ref, xbc_ref, dt_ref,
                   u_ref, vg_ref, gates_ref):
    xn = _rms(h_ref[...], pre_ref[...]).astype(BF16)
    outs = (q_ref, k_ref, v_ref, z_ref, xbc_ref, dt_ref, u_ref, vg_ref, gates_ref)
    off = 0
    for idx, o_ref in enumerate(outs):
        width = o_ref.shape[1]
        for c0 in range(0, width, INPROJ_COL_CHUNK):
            n = min(INPROJ_COL_CHUNK, width - c0)
            r = _dot(xn, w_ref[:, off + c0:off + c0 + n])
            if idx == 0:
                r = r * (ATTN_HEAD_DIM ** -0.5)
            o_ref[:, c0:c0 + n] = r.astype(o_ref.dtype)
        off += width


def _inproj(h, pre_g, w_pad):
    t, d = h.shape
    tm = INPROJ_TOKENS
    layout = _inproj_layout(d)
    assert t % tm == 0 and w_pad.shape[1] == sum(p for _, _, p in layout)
    out_shape = tuple(jax.ShapeDtypeStruct((t, p), F32 if name == "dt" else BF16)
                      for name, _, p in layout)
    out_specs = tuple(pl.BlockSpec((tm, p), lambda i: (i, 0)) for _, _, p in layout)
    return pl.pallas_call(
        _inproj_kernel,
        out_shape=out_shape,
        grid=(t // tm,),
        in_specs=[
            pl.BlockSpec((tm, d), lambda i: (i, 0)),
            _resident((1, d)),
            _resident(w_pad.shape),
        ],
        out_specs=out_specs,
        compiler_params=pltpu.CompilerParams(
            dimension_semantics=("arbitrary",), vmem_limit_bytes=VMEM_LIMIT),
        name="inproj",
    )(h, pre_g.reshape(1, d), w_pad)


def _pad_w_in(w_in):
    d = w_in.shape[0]
    pieces, off = [], 0
    for _, width, padded in _inproj_layout(d):
        pieces.append(w_in[:, off:off + width])
        if padded > width:
            pieces.append(jnp.zeros((d, padded - width), w_in.dtype))
        off += width
    assert off == w_in.shape[1]
    return jnp.concatenate(pieces, axis=1).astype(BF16)


def _attn_kernel(slope_ref, q_ref, k_ref, v_ref, o_ref, vt_ref, kmean_ref, sel_ref, bias_ref):
    blk = MOBA_BLOCK
    seq = q_ref.shape[1]
    nb = seq // blk
    slope = slope_ref[0, 0:1, :]

    key_in_blk = lax.broadcasted_iota(jnp.int32, (blk, blk), 0)
    qry_in_blk = lax.broadcasted_iota(jnp.int32, (blk, blk), 1)
    slope_row = jnp.concatenate([slope] * (blk // LANES), axis=1)
    bias_ref[...] = key_in_blk.astype(F32) * slope_row

    def prologue(jb, carry):
        off = pl.multiple_of(jb * blk, blk)
        kb = k_ref[0, pl.ds(off, blk), :].astype(F32)
        kmean_ref[pl.ds(jb, 1), :] = jnp.sum(kb, axis=0, keepdims=True) * (1.0 / blk)
        vb = v_ref[0, pl.ds(off, blk), :].astype(F32)
        vt_ref[:, pl.ds(off, blk)] = vb.T.astype(BF16)
        return carry

    lax.fori_loop(0, nb, prologue, 0)

    km_hi, km_lo = _split2(kmean_ref[...])
    blk_ids = lax.broadcasted_iota(jnp.int32, (nb, blk), 0).astype(F32)

    def q_tile(i, carry):
        qoff = pl.multiple_of(i * blk, blk)
        qi = q_ref[0, pl.ds(qoff, blk), :]

        gate = _dot_nt(km_hi, qi) + _dot_nt(km_lo, qi)
        gate = jnp.where(blk_ids < i.astype(F32), gate, -jnp.inf)
        sel = jnp.zeros((nb, blk), F32)
        for _ in range(MOBA_TOPK):
            mx = jnp.max(gate, axis=0, keepdims=True)
            first = jnp.min(jnp.where(gate == mx, blk_ids, float(nb)), axis=0, keepdims=True)
            hit = blk_ids == first
            sel = jnp.where(jnp.logical_and(hit, mx > -jnp.inf), 1.0, sel)
            gate = jnp.where(hit, -jnp.inf, gate)
        sel_ref[...] = sel

        st = _dot_nt(k_ref[0, pl.ds(qoff, blk), :], qi) + bias_ref[...]
        st = jnp.where(key_in_blk <= qry_in_blk, st, NEG_BIG)
        m0 = jnp.max(st, axis=0, keepdims=True)
        p = jnp.exp(st - m0)
        l0 = jnp.sum(p, axis=0, keepdims=True)
        acc0 = _dot(vt_ref[:, pl.ds(qoff, blk)], p.astype(BF16))

        def past_block(j, mla):
            m, l, acc = mla
            koff = pl.multiple_of(j * blk, blk)
            st = _dot_nt(k_ref[0, pl.ds(koff, blk), :], qi) + bias_ref[...]
            cj = slope_row * ((j - i) * blk).astype(F32)
            chosen = sel_ref[pl.ds(j, 1), :] > 0.0
            cmax = jnp.max(st, axis=0, keepdims=True) + cj
            m_new = jnp.where(chosen, jnp.maximum(m, cmax), m)
            shift = jnp.where(chosen, m_new - cj, POS_BIG)
            p = jnp.exp(st - shift)
            alpha = jnp.exp(m - m_new)
            l_new = alpha * l + jnp.sum(p, axis=0, keepdims=True)
            acc_new = alpha * acc + _dot(vt_ref[:, pl.ds(koff, blk)], p.astype(BF16))
            return m_new, l_new, acc_new

        _, l, acc = lax.fori_loop(0, i, past_block, (m0, l0, acc0))
        o_ref[0, pl.ds(qoff, blk), :] = (acc / l).T.astype(o_ref.dtype)
        return carry

    lax.fori_loop(0, nb, q_tile, 0)


def _attention(q, k, v):
    b, s, w = q.shape
    dh = ATTN_HEAD_DIM
    nh = w // dh
    assert s % MOBA_BLOCK == 0 and s // MOBA_BLOCK >= MOBA_TOPK and dh == LANES
    nb = s // MOBA_BLOCK
    slopes = 2.0 ** (-8.0 * (jnp.arange(nh, dtype=F32) + 1.0) / nh)
    slopes = jnp.broadcast_to(slopes[:, None, None], (nh, SUBLANES, LANES))
    head_spec = pl.BlockSpec((1, s, dh), lambda bi, hi: (bi, 0, hi))
    return pl.pallas_call(
        _attn_kernel,
        out_shape=jax.ShapeDtypeStruct((b, s, w), BF16),
        grid=(b, nh),
        in_specs=[pl.BlockSpec((1, SUBLANES, LANES), lambda bi, hi: (hi, 0, 0)),
                  head_spec, head_spec, head_spec],
        out_specs=head_spec,
        scratch_shapes=[
            pltpu.VMEM((dh, s), BF16),
            pltpu.VMEM((nb, dh), F32),
            pltpu.VMEM((nb, MOBA_BLOCK), F32),
            pltpu.VMEM((MOBA_BLOCK, MOBA_BLOCK), F32),
        ],
        compiler_params=pltpu.CompilerParams(
            dimension_semantics=("arbitrary", "arbitrary"), vmem_limit_bytes=VMEM_LIMIT),
        name="moba_attention",
    )(slopes, q, k, v)


def _ssm_kernel(xbc_ref, dt_ref, z_ref, convw_ref, convb_ref, dtb_ref, alog_ref, dskip_ref,
                normg_ref, expand_ref, o_ref, xraw_ref, xc_ref, state_ref, y_ref):
    ts = xbc_ref.shape[1]
    q = SSM_CHUNK
    hp = SSM_HEAD_DIM
    gw = SSM_WIDTH // SSM_GROUPS
    hpg = SSM_HEADS // SSM_GROUPS
    pad = SUBLANES

    @pl.when(pl.program_id(1) == 0)
    def _():
        xraw_ref[0:pad, :] = jnp.zeros((pad, SSM_CONV_CH), F32)
        state_ref[...] = jnp.zeros_like(state_ref)

    xraw_ref[pad:pad + ts, :] = xbc_ref[0].astype(F32)
    conv = convb_ref[...] + convw_ref[SSM_CONV - 1:SSM_CONV, :] * xraw_ref[pad:pad + ts, :]
    for kk in range(SSM_CONV - 1):
        lag = SSM_CONV - 1 - kk
        conv = conv + convw_ref[kk:kk + 1, :] * xraw_ref[pad - lag:pad - lag + ts, :]
    xc_ref[...] = conv * _sigmoid(conv)
    xraw_ref[0:pad, :] = xraw_ref[ts:ts + pad, :]

    a_neg = -jnp.exp(alog_ref[...])
    expand = expand_ref[...]
    row = lax.broadcasted_iota(jnp.int32, (q, q), 0)
    col = lax.broadcasted_iota(jnp.int32, (q, q), 1)
    lower = row >= col
    tri = jnp.where(lower, 1.0, 0.0).astype(BF16)

    def expand_heads(w):
        hi, lo = _split2(w)
        return _dot(hi, expand) + _dot(lo, expand)

    def chunk(c, carry):
        r0 = pl.multiple_of(c * q, q)
        rows = pl.ds(r0, q)
        xs = xc_ref[rows, 0:SSM_WIDTH]
        xs_b = xs.astype(BF16)
        dtr = dt_ref[0, rows, :] + dtb_ref[...]
        dt = jnp.maximum(dtr, 0.0) + jnp.log1p(jnp.exp(-jnp.abs(dtr)))
        da = dt * a_neg
        d_hi, d_mid, d_lo = _split3(da)
        a_cs = _dot(tri, d_hi) + _dot(tri, d_mid) + _dot(tri, d_lo)
        a_cs_t = a_cs.T
        dt_t = dt.T
        a_last = a_cs[q - 1:q, :]

        ea_x = expand_heads(jnp.exp(a_cs))
        wend_x = expand_heads(jnp.exp(a_last - a_cs) * dt)
        cdec_x = ea_x[q - 1:q, :]
        xs_w = (xs * wend_x).astype(BF16)

        for g in range(SSM_GROUPS):
            bg = xc_ref[rows, SSM_WIDTH + g * SSM_STATE:SSM_WIDTH + (g + 1) * SSM_STATE].astype(BF16)
            cg = xc_ref[rows, SSM_WIDTH + SSM_BC + g * SSM_STATE:
                        SSM_WIDTH + SSM_BC + (g + 1) * SSM_STATE].astype(BF16)
            cb = _dot_nt(cg, bg)
            lanes = slice(g * gw, (g + 1) * gw)
            st = state_ref[:, lanes]
            y_ref[:, lanes] = _dot(cg, st.astype(BF16)) * ea_x[:, lanes]
            state_ref[:, lanes] = st * cdec_x[:, lanes] + _dot_tn(bg, xs_w[:, lanes])
            for hh in range(0, hpg, 2):
                pair = []
                for h in (g * hpg + hh, g * hpg + hh + 1):
                    seg = a_cs[:, h:h + 1] - a_cs_t[h:h + 1, :]
                    wts = cb * jnp.where(lower, jnp.exp(seg), 0.0) * dt_t[h:h + 1, :]
                    pair.append(_dot(wts.astype(BF16), xs_b[:, h * hp:(h + 1) * hp]))
                h0 = (g * hpg + hh) * hp
                y_ref[:, h0:h0 + 2 * hp] += jnp.concatenate(pair, axis=1)

        y = y_ref[...] + dskip_ref[...] * xs
        zz = z_ref[0, rows, :].astype(F32)
        y = y * (zz * _sigmoid(zz))
        o_ref[0, rows, :] = _rms(y, normg_ref[...]).astype(o_ref.dtype)
        return carry

    lax.fori_loop(0, ts // q, chunk, 0)


def _ssm(xbc, dt_raw, z, conv_w, conv_b, dt_bias, a_log, d_skip, norm_g):
    b, s, _ = xbc.shape
    ts = min(SSM_TOKENS, s)
    assert s % ts == 0 and ts % SSM_CHUNK == 0 and SSM_HEADS <= LANES

    def lane_pad(x):
        return jnp.zeros((1, LANES), F32).at[0, :SSM_HEADS].set(x)

    head_of_lane = jnp.arange(SSM_WIDTH) // SSM_HEAD_DIM
    expand = (jnp.arange(LANES)[:, None] == head_of_lane[None, :]).astype(BF16)
    tile = lambda w: pl.BlockSpec((1, ts, w), lambda bi, ti: (bi, ti, 0))
    return pl.pallas_call(
        _ssm_kernel,
        out_shape=jax.ShapeDtypeStruct((b, s, SSM_WIDTH), BF16),
        grid=(b, s // ts),
        in_specs=[
            tile(SSM_CONV_CH), tile(LANES), tile(SSM_WIDTH),
            _resident((SSM_CONV, SSM_CONV_CH)), _resident((1, SSM_CONV_CH)),
            _resident((1, LANES)), _resident((1, LANES)),
            _resident((1, SSM_WIDTH)), _resident((1, SSM_WIDTH)),
            _resident((LANES, SSM_WIDTH)),
        ],
        out_specs=tile(SSM_WIDTH),
        scratch_shapes=[
            pltpu.VMEM((ts + SUBLANES, SSM_CONV_CH), F32),
            pltpu.VMEM((ts, SSM_CONV_CH), F32),
            pltpu.VMEM((SSM_STATE, SSM_WIDTH), F32),
            pltpu.VMEM((SSM_CHUNK, SSM_WIDTH), F32),
        ],
        compiler_params=pltpu.CompilerParams(
            dimension_semantics=("arbitrary", "arbitrary"), vmem_limit_bytes=VMEM_LIMIT),
        name="ssd",
    )(xbc, dt_raw, z, conv_w, conv_b.reshape(1, -1), lane_pad(dt_bias), lane_pad(a_log),
      jnp.repeat(d_skip, SSM_HEAD_DIM).reshape(1, -1), norm_g.reshape(1, -1), expand)


def _merge_kernel(h_ref, ya_ref, ys_ref, u_ref, vg_ref, gates_ref, lng_ref, lnb_ref, sw_ref,
                  sbt_ref, pa_ref, ps_ref, pc_ref, wo_ref, post_ref, o_ref, yc_ref):
    tm, d = h_ref.shape
    ch, gd = SGU_CHUNK, SGU_GROUP_DIM

    vg = _gelu_tanh(vg_ref[...].astype(F32))
    mu = jnp.mean(vg, axis=-1, keepdims=True)
    var = jnp.mean(jnp.square(vg - mu), axis=-1, keepdims=True)
    vn = ((vg - mu) * lax.rsqrt(var + NORM_EPS) * lng_ref[...] + lnb_ref[...]).astype(BF16)

    row = lax.broadcasted_iota(jnp.int32, (ch, ch), 0)
    col = lax.broadcasted_iota(jnp.int32, (ch, ch), 1)
    for g in range(SGU_GROUPS):
        w_sp = jnp.where(row >= col, sw_ref[g], 0.0).astype(BF16)
        bias = sbt_ref[:, g:g + 1]
        for c in range(tm // ch):
            rs, cs = slice(c * ch, (c + 1) * ch), slice(g * gd, (g + 1) * gd)
            sv = _dot(w_sp, vn[rs, cs]) + bias
            yc_ref[rs, cs] = (_gelu_tanh(u_ref[rs, cs].astype(F32)) * sv).astype(BF16)

    gates = gates_ref[...].astype(F32)
    merged = _sigmoid(gates[:, 0:d]) * _dot(ya_ref[...], pa_ref[...])
    merged += _sigmoid(gates[:, d:2 * d]) * _dot(ys_ref[...], ps_ref[...])
    merged += _sigmoid(gates[:, 2 * d:3 * d]) * _dot(yc_ref[...], pc_ref[...])
    m = _dot(merged.astype(BF16), wo_ref[...])
    o_ref[...] = h_ref[...] + _rms(m, post_ref[...])


def _merge(h, ya, ys, u, vg, gates, ln_g, ln_b, sgu_w, sgu_b, p_attn, p_ssm, p_sgu, w_out, post_g):
    t, d = h.shape
    tm = MERGE_TOKENS
    sw = SGU_GROUPS * SGU_GROUP_DIM
    assert t % tm == 0 and tm % SGU_CHUNK == 0
    tile = lambda w: pl.BlockSpec((tm, w), lambda i: (i, 0))
    return pl.pallas_call(
        _merge_kernel,
        out_shape=jax.ShapeDtypeStruct((t, d), F32),
        grid=(t // tm,),
        in_specs=[
            tile(d), tile(ya.shape[1]), tile(ys.shape[1]), tile(sw), tile(sw), tile(N_BRANCHES * d),
            _resident((1, sw)), _resident((1, sw)),
            _resident(sgu_w.shape), _resident((SGU_CHUNK, SGU_GROUPS)),
            _resident(p_attn.shape), _resident(p_ssm.shape), _resident(p_sgu.shape),
            _resident(w_out.shape), _resident((1, d)),
        ],
        out_specs=tile(d),
        scratch_shapes=[pltpu.VMEM((tm, sw), BF16)],
        compiler_params=pltpu.CompilerParams(
            dimension_semantics=("arbitrary",), vmem_limit_bytes=VMEM_LIMIT),
        name="merge",
    )(h, ya, ys, u, vg, gates, ln_g.reshape(1, -1), ln_b.reshape(1, -1), sgu_w, sgu_b.T,
      p_attn, p_ssm, p_sgu, w_out, post_g.reshape(1, d))


def kernel(x, ffn1_pre_g, ffn1_w_gu, ffn1_w_down, ffn1_post_g, mix_pre_g, w_in, conv_w, conv_b, dt_bias, a_log, d_skip, ssm_norm_g, sgu_ln_g, sgu_ln_b, sgu_w, sgu_b, p_attn, p_ssm, p_sgu, w_out, mix_post_g, ffn2_pre_g, ffn2_w_gu, ffn2_w_down, ffn2_post_g):
    b, s, d = x.shape
    depth = w_in.shape[0]
    h = x.reshape(b * s, d)
    as_b = lambda a: a.astype(BF16)
    for i in range(depth):
        h = _ffn(h, ffn1_pre_g[i], as_b(ffn1_w_gu[i]), as_b(ffn1_w_down[i]), ffn1_post_g[i])
        q, k, v, z, xbc, dt_raw, u, vg, gates = _inproj(h, mix_pre_g[i], _pad_w_in(w_in[i]))
        seq = lambda a: a.reshape(b, s, a.shape[-1])
        ya = _attention(seq(q), seq(k), seq(v))
        ys = _ssm(seq(xbc), seq(dt_raw), seq(z), conv_w[i], conv_b[i], dt_bias[i], a_log[i],
                  d_skip[i], ssm_norm_g[i])
        h = _merge(h, ya.reshape(b * s, -1), ys.reshape(b * s, -1), u, vg, gates,
                   sgu_ln_g[i], sgu_ln_b[i], sgu_w[i], sgu_b[i], as_b(p_attn[i]), as_b(p_ssm[i]),
                   as_b(p_sgu[i]), as_b(w_out[i]), mix_post_g[i])
        h = _ffn(h, ffn2_pre_g[i], as_b(ffn2_w_gu[i]), as_b(ffn2_w_down[i]), ffn2_post_g[i])
    return h.reshape(b, s, d)
```

```python
import functools

import jax
import jax.numpy as jnp
from jax import lax
from jax.experimental import pallas as pl
from jax.experimental.pallas import tpu as pltpu

F32 = jnp.float32
BF16 = jnp.bfloat16

NORM_EPS = 1e-6
ATTN_HEADS = 8
ATTN_HEAD_DIM = 128
MOBA_BLOCK = 256
MOBA_TOPK = 3
SSM_HEADS = 16
SSM_HEAD_DIM = 64
SSM_WIDTH = SSM_HEADS * SSM_HEAD_DIM
SSM_GROUPS = 2
SSM_STATE = 128
SSM_CONV = 4
SSM_CHUNK = 128
SSM_BC = SSM_GROUPS * SSM_STATE
SSM_CONV_CH = SSM_WIDTH + 2 * SSM_BC
SGU_GROUPS = 8
SGU_GROUP_DIM = 128
SGU_CHUNK = 128
N_BRANCHES = 3

LANES = 128
SUBLANES = 8
VMEM_LIMIT = 56 * 1024 * 1024
NEG_BIG = -1e30
POS_BIG = 1e30

FFN_TOKENS = 512
FFN_HIDDEN_CHUNK = 256
INPROJ_TOKENS = 256
INPROJ_COL_CHUNK = 512
SSM_TOKENS = 512
MERGE_TOKENS = 256
ATTN_GROUP = 8


def _resident(shape):
    nd = len(shape)
    return pl.BlockSpec(shape, lambda *_: (0,) * nd, pipeline_mode=pl.Buffered(1))


def _rms(x, g):
    return x * lax.rsqrt(jnp.mean(x * x, axis=-1, keepdims=True) + NORM_EPS) * g


def _sigmoid(x):
    return 1.0 / (1.0 + jnp.exp(-x))


def _gelu_tanh(x):
    return x * (0.5 * (1.0 + jnp.tanh(0.7978845608028654 * (x + 0.044715 * (x * x * x)))))


def _dot(a, b):
    return jnp.dot(a, b, preferred_element_type=F32)


def _dot_nt(a, b):
    return lax.dot_general(a, b, (((1,), (1,)), ((), ())), preferred_element_type=F32)


def _dot_tn(a, b):
    return lax.dot_general(a, b, (((0,), (0,)), ((), ())), preferred_element_type=F32)


def _split2(x):
    hi = x.astype(BF16)
    lo = (x - hi.astype(F32)).astype(BF16)
    return hi, lo


def _split3(x):
    hi = x.astype(BF16)
    r = x - hi.astype(F32)
    mid = r.astype(BF16)
    lo = (r - mid.astype(F32)).astype(BF16)
    return hi, mid, lo


def _ffn_kernel(h_ref, pre_ref, wgu_ref, wd_ref, post_ref, o_ref, acc_ref):
    x = h_ref[...]
    xn = _rms(x, pre_ref[...]).astype(BF16)
    hidden = wd_ref.shape[0]
    fc = FFN_HIDDEN_CHUNK
    for c in range(hidden // fc):
        g = _dot(xn, wgu_ref[:, c * fc:(c + 1) * fc])
        u = _dot(xn, wgu_ref[:, hidden + c * fc:hidden + (c + 1) * fc])
        a = (g * _sigmoid(g) * u).astype(BF16)
        d = _dot(a, wd_ref[c * fc:(c + 1) * fc, :])
        if c == 0:
            acc_ref[...] = d
        else:
            acc_ref[...] += d
    o_ref[...] = x + 0.5 * _rms(acc_ref[...], post_ref[...])


def _ffn(h, pre_g, w_gu, w_down, post_g):
    t, d = h.shape
    hidden = w_down.shape[0]
    assert hidden % FFN_HIDDEN_CHUNK == 0 and t % FFN_TOKENS == 0
    tm = FFN_TOKENS
    return pl.pallas_call(
        _ffn_kernel,
        out_shape=jax.ShapeDtypeStruct((t, d), F32),
        grid=(t // tm,),
        in_specs=[
            pl.BlockSpec((tm, d), lambda i: (i, 0)),
            _resident((1, d)),
            _resident((d, 2 * hidden)),
            _resident((hidden, d)),
            _resident((1, d)),
        ],
        out_specs=pl.BlockSpec((tm, d), lambda i: (i, 0)),
        scratch_shapes=[pltpu.VMEM((tm, d), F32)],
        compiler_params=pltpu.CompilerParams(
            dimension_semantics=("arbitrary",), vmem_limit_bytes=VMEM_LIMIT),
        name="ffn",
    )(h, pre_g.reshape(1, d), w_gu, w_down, post_g.reshape(1, d))


def _inproj_layout(d_model):
    aw = ATTN_HEADS * ATTN_HEAD_DIM
    sw = SGU_GROUPS * SGU_GROUP_DIM
    return (("q", aw, aw), ("k", aw, aw), ("v", aw, aw), ("z", SSM_WIDTH, SSM_WIDTH),
            ("xbc", SSM_CONV_CH, SSM_CONV_CH), ("dt", SSM_HEADS, LANES),
            ("u", sw, sw), ("vg", sw, sw), ("gates", N_BRANCHES * d_model, N_BRANCHES * d_model))


def _inproj_kernel(h_ref, pre_ref, w_ref, q_ref, k_ref, v_ref, z_ref, xbc_ref, dt_ref,
                   u_ref, vg_ref, gates_ref):
    xn = _rms(h_ref[...], pre_ref[...]).astype(BF16)
    outs = (q_ref, k_ref, v_ref, z_ref, xbc_ref, dt_ref, u_ref, vg_ref, gates_ref)
    off = 0
    for idx, o_ref in enumerate(outs):
        width = o_ref.shape[1]
        for c0 in range(0, width, INPROJ_COL_CHUNK):
            n = min(INPROJ_COL_CHUNK, width - c0)
            r = _dot(xn, w_ref[:, off + c0:off + c0 + n])
            if idx == 0:
                r = r * (ATTN_HEAD_DIM ** -0.5)
            o_ref[:, c0:c0 + n] = r.astype(o_ref.dtype)
        off += width


def _inproj(h, pre_g, w_pad):
    t, d = h.shape
    tm = INPROJ_TOKENS
    layout = _inproj_layout(d)
    assert t % tm == 0 and w_pad.shape[1] == sum(p for _, _, p in layout)
    out_shape = tuple(jax.ShapeDtypeStruct((t, p), F32 if name == "dt" else BF16)
                      for name, _, p in layout)
    out_specs = tuple(pl.BlockSpec((tm, p), lambda i: (i, 0)) for _, _, p in layout)
    return pl.pallas_call(
        _inproj_kernel,
        out_shape=out_shape,
        grid=(t // tm,),
        in_specs=[
            pl.BlockSpec((tm, d), lambda i: (i, 0)),
            _resident((1, d)),
            _resident(w_pad.shape),
        ],
        out_specs=out_specs,
        compiler_params=pltpu.CompilerParams(
            dimension_semantics=("arbitrary",), vmem_limit_bytes=VMEM_LIMIT),
        name="inproj",
    )(h, pre_g.reshape(1, d), w_pad)


def _pad_w_in(w_in):
    d = w_in.shape[0]
    pieces, off = [], 0
    for _, width, padded in _inproj_layout(d):
        pieces.append(w_in[:, off:off + width])
        if padded > width:
            pieces.append(jnp.zeros((d, padded - width), w_in.dtype))
        off += width
    assert off == w_in.shape[1]
    return jnp.concatenate(pieces, axis=1).astype(BF16)


def _attn_kernel(slope_ref, q_ref, k_ref, v_ref, o_ref, vt_ref, kmean_ref, sel_ref, bias_ref,
                 s_ref, cmax_ref, p_ref):
    blk = MOBA_BLOCK
    seq = q_ref.shape[1]
    nb = seq // blk
    gk = bias_ref.shape[0]
    group = gk // blk
    slope = slope_ref[0, 0:1, :]

    key_in_blk = lax.broadcasted_iota(jnp.int32, (blk, blk), 0)
    qry_in_blk = lax.broadcasted_iota(jnp.int32, (blk, blk), 1)
    slope_row = jnp.concatenate([slope] * (blk // LANES), axis=1)
    bias_ref[...] = lax.broadcasted_iota(jnp.int32, (gk, blk), 0).astype(F32) * slope_row
    cmax_ref[...] = jnp.zeros_like(cmax_ref)

    def prologue(jb, carry):
        off = pl.multiple_of(jb * blk, blk)
        kb = k_ref[0, pl.ds(off, blk), :].astype(F32)
        kmean_ref[pl.ds(jb, 1), :] = jnp.sum(kb, axis=0, keepdims=True) * (1.0 / blk)
        vb = v_ref[0, pl.ds(off, blk), :].astype(F32)
        vt_ref[:, pl.ds(off, blk)] = vb.T.astype(BF16)
        return carry

    lax.fori_loop(0, nb, prologue, 0)

    km_hi, km_lo = _split2(kmean_ref[...])
    blk_ids = lax.broadcasted_iota(jnp.int32, (nb, blk), 0).astype(F32)

    def q_tile(i, carry):
        qoff = pl.multiple_of(i * blk, blk)
        qi = q_ref[0, pl.ds(qoff, blk), :]

        gate = _dot_nt(km_hi, qi) + _dot_nt(km_lo, qi)
        gate = jnp.where(blk_ids < i.astype(F32), gate, -jnp.inf)
        sel = jnp.zeros((nb, blk), F32)
        for _ in range(MOBA_TOPK):
            mx = jnp.max(gate, axis=0, keepdims=True)
            first = jnp.min(jnp.where(gate == mx, blk_ids, float(nb)), axis=0, keepdims=True)
            hit = blk_ids == first
            sel = jnp.where(jnp.logical_and(hit, mx > -jnp.inf), 1.0, sel)
            gate = jnp.where(hit, -jnp.inf, gate)
        sel_ref[...] = sel

        st = _dot_nt(k_ref[0, pl.ds(qoff, blk), :], qi) + bias_ref[0:blk, :]
        st = jnp.where(key_in_blk <= qry_in_blk, st, NEG_BIG)
        m_own = jnp.max(st, axis=0, keepdims=True)

        ngroups = (i + group - 1) // group

        def group_offset(g):
            return slope_row * (g * gk - i * blk).astype(F32)

        def score(g, c):
            r0 = pl.multiple_of(g * gk, gk)
            s = _dot_nt(k_ref[0, pl.ds(r0, gk), :], qi) + bias_ref[...]
            s_ref[pl.ds(r0, gk), :] = s
            cg = group_offset(g)
            for u in range(group):
                cm = jnp.max(s[u * blk:(u + 1) * blk, :], axis=0, keepdims=True)
                cmax_ref[pl.ds(g * group + u, 1), :] = cm + cg
            return c

        lax.fori_loop(0, ngroups, score, 0)
        m = jnp.maximum(m_own, jnp.max(jnp.where(sel > 0.0, cmax_ref[...], NEG_BIG),
                                       axis=0, keepdims=True))

        p = jnp.exp(st - m)
        l0 = jnp.sum(p, axis=0, keepdims=True)
        acc0 = _dot(vt_ref[:, pl.ds(qoff, blk)], p.astype(BF16))

        def accumulate(g, la):
            l, acc = la
            r0 = pl.multiple_of(g * gk, gk)
            base = m - group_offset(g)
            for u in range(group):
                chosen = sel_ref[pl.ds(g * group + u, 1), :] > 0.0
                shift = jnp.where(chosen, base, POS_BIG)
                pu = jnp.exp(s_ref[pl.ds(r0 + u * blk, blk), :] - shift)
                l = l + jnp.sum(pu, axis=0, keepdims=True)
                p_ref[u * blk:(u + 1) * blk, :] = pu.astype(BF16)
            acc = acc + _dot(vt_ref[:, pl.ds(r0, gk)], p_ref[...])
            return l, acc

        l, acc = lax.fori_loop(0, ngroups, accumulate, (l0, acc0))
        o_ref[0, pl.ds(qoff, blk), :] = (acc / l).T.astype(o_ref.dtype)
        return carry

    lax.fori_loop(0, nb, q_tile, 0)


def _attention(q, k, v):
    b, s, w = q.shape
    dh = ATTN_HEAD_DIM
    nh = w // dh
    assert s % MOBA_BLOCK == 0 and s // MOBA_BLOCK >= MOBA_TOPK and dh == LANES
    nb = s // MOBA_BLOCK
    group = min(ATTN_GROUP, nb)
    assert nb % group == 0
    gk = group * MOBA_BLOCK
    slopes = 2.0 **(-8.0 * (jnp.arange(nh, dtype=F32) + 1.0) / nh)
    slopes = jnp.broadcast_to(slopes[:, None, None], (nh, SUBLANES, LANES))
    head_spec = pl.BlockSpec((1, s, dh), lambda bi, hi: (bi, 0, hi))
    return pl.pallas_call(
        _attn_kernel,
        out_shape=jax.ShapeDtypeStruct((b, s, w), BF16),
        grid=(b, nh),
        in_specs=[pl.BlockSpec((1, SUBLANES, LANES), lambda bi, hi: (hi, 0, 0)),
                  head_spec, head_spec, head_spec],
        out_specs=head_spec,
        scratch_shapes=[
            pltpu.VMEM((dh, s), BF16),
            pltpu.VMEM((nb, dh), F32),
            pltpu.VMEM((nb, MOBA_BLOCK), F32),
            pltpu.VMEM((gk, MOBA_BLOCK), F32),
            pltpu.VMEM((s, MOBA_BLOCK), F32),
            pltpu.VMEM((nb, MOBA_BLOCK), F32),
            pltpu.VMEM((gk, MOBA_BLOCK), BF16),
        ],
        compiler_params=pltpu.CompilerParams(
            dimension_semantics=("arbitrary", "arbitrary"), vmem_limit_bytes=VMEM_LIMIT),
        name="moba_attention",
    )(slopes, q, k, v)


def _ssm_kernel(xbc_ref, dt_ref, z_ref, convw_ref, convb_ref, dtb_ref, alog_ref, dskip_ref,
                normg_ref, expand_ref, o_ref, xraw_ref, xc_ref, state_ref, y_ref):
    ts = xbc_ref.shape[1]
    q = SSM_CHUNK
    hp = SSM_HEAD_DIM
    gw = SSM_WIDTH // SSM_GROUPS
    hpg = SSM_HEADS // SSM_GROUPS
    pad = SUBLANES

    @pl.when(pl.program_id(1) == 0)
    def _():
        xraw_ref[0:pad, :] = jnp.zeros((pad, SSM_CONV_CH), F32)
        state_ref[...] = jnp.zeros_like(state_ref)

    xraw_ref[pad:pad + ts, :] = xbc_ref[0].astype(F32)
    conv = convb_ref[...] + convw_ref[SSM_CONV - 1:SSM_CONV, :] * xraw_ref[pad:pad + ts, :]
    for kk in range(SSM_CONV - 1):
        lag = SSM_CONV - 1 - kk
        conv = conv + convw_ref[kk:kk + 1, :] * xraw_ref[pad - lag:pad - lag + ts, :]
    xc_ref[...] = conv * _sigmoid(conv)
    xraw_ref[0:pad, :] = xraw_ref[ts:ts + pad, :]

    a_neg = -jnp.exp(alog_ref[...])
    expand = expand_ref[...]
    row = lax.broadcasted_iota(jnp.int32, (q, q), 0)
    col = lax.broadcasted_iota(jnp.int32, (q, q), 1)
    lower = row >= col
    tri = jnp.where(lower, 1.0, 0.0).astype(BF16)

    def expand_heads(w):
        hi, lo = _split2(w)
        return _dot(hi, expand) + _dot(lo, expand)

    def chunk(c, carry):
        r0 = pl.multiple_of(c * q, q)
        rows = pl.ds(r0, q)
        xs = xc_ref[rows, 0:SSM_WIDTH]
        xs_b = xs.astype(BF16)
        dtr = dt_ref[0, rows, :] + dtb_ref[...]
        dt = jnp.maximum(dtr, 0.0) + jnp.log1p(jnp.exp(-jnp.abs(dtr)))
        da = dt * a_neg
        d_hi, d_mid, d_lo = _split3(da)
        a_cs = _dot(tri, d_hi) + _dot(tri, d_mid) + _dot(tri, d_lo)
        a_cs_t = a_cs.T
        dt_t = dt.T
        a_last = a_cs[q - 1:q, :]

        ea_x = expand_heads(jnp.exp(a_cs))
        wend_x = expand_heads(jnp.exp(a_last - a_cs) * dt)
        cdec_x = ea_x[q - 1:q, :]
        xs_w = (xs * wend_x).astype(BF16)

        for g in range(SSM_GROUPS):
            bg = xc_ref[rows, SSM_WIDTH + g * SSM_STATE:SSM_WIDTH + (g + 1) * SSM_STATE].astype(BF16)
            cg = xc_ref[rows, SSM_WIDTH + SSM_BC + g * SSM_STATE:
                        SSM_WIDTH + SSM_BC + (g + 1) * SSM_STATE].astype(BF16)
            cb = _dot_nt(cg, bg)
            lanes = slice(g * gw, (g + 1) * gw)
            st = state_ref[:, lanes]
            y_ref[:, lanes] = _dot(cg, st.astype(BF16)) * ea_x[:, lanes]
            state_ref[:, lanes] = st * cdec_x[:, lanes] + _dot_tn(bg, xs_w[:, lanes])
            for hh in range(0, hpg, 2):
                pair = []
                for h in (g * hpg + hh, g * hpg + hh + 1):
                    seg = a_cs[:, h:h + 1] - a_cs_t[h:h + 1, :]
                    wts = cb * jnp.where(lower, jnp.exp(seg), 0.0) * dt_t[h:h + 1, :]
                    pair.append(_dot(wts.astype(BF16), xs_b[:, h * hp:(h + 1) * hp]))
                h0 = (g * hpg + hh) * hp
                y_ref[:, h0:h0 + 2 * hp] += jnp.concatenate(pair, axis=1)

        y = y_ref[...] + dskip_ref[...] * xs
        zz = z_ref[0, rows, :].astype(F32)
        y = y * (zz * _sigmoid(zz))
        o_ref[0, rows, :] = _rms(y, normg_ref[...]).astype(o_ref.dtype)
        return carry

    lax.fori_loop(0, ts // q, chunk, 0)


def _ssm(xbc, dt_raw, z, conv_w, conv_b, dt_bias, a_log, d_skip, norm_g):
    b, s, _ = xbc.shape
    ts = min(SSM_TOKENS, s)
    assert s % ts == 0 and ts % SSM_CHUNK == 0 and SSM_HEADS <= LANES

    def lane_pad(x):
        return jnp.zeros((1, LANES), F32).at[0, :SSM_HEADS].set(x)

    head_of_lane = jnp.arange(SSM_WIDTH) // SSM_HEAD_DIM
    expand = (jnp.arange(LANES)[:, None] == head_of_lane[None, :]).astype(BF16)
    tile = lambda w: pl.BlockSpec((1, ts, w), lambda bi, ti: (bi, ti, 0))
    return pl.pallas_call(
        _ssm_kernel,
        out_shape=jax.ShapeDtypeStruct((b, s, SSM_WIDTH), BF16),
        grid=(b, s // ts),
        in_specs=[
            tile(SSM_CONV_CH), tile(LANES), tile(SSM_WIDTH),
            _resident((SSM_CONV, SSM_CONV_CH)), _resident((1, SSM_CONV_CH)),
            _resident((1, LANES)), _resident((1, LANES)),
            _resident((1, SSM_WIDTH)), _resident((1, SSM_WIDTH)),
            _resident((LANES, SSM_WIDTH)),
        ],
        out_specs=tile(SSM_WIDTH),
        scratch_shapes=[
            pltpu.VMEM((ts + SUBLANES, SSM_CONV_CH), F32),
            pltpu.VMEM((ts, SSM_CONV_CH), F32),
            pltpu.VMEM((SSM_STATE, SSM_WIDTH), F32),
            pltpu.VMEM((SSM_CHUNK, SSM_WIDTH), F32),
        ],
        compiler_params=pltpu.CompilerParams(
            dimension_semantics=("arbitrary", "arbitrary"), vmem_limit_bytes=VMEM_LIMIT),
        name="ssd",
    )(xbc, dt_raw, z, conv_w, conv_b.reshape(1, -1), lane_pad(dt_bias), lane_pad(a_log),
      jnp.repeat(d_skip, SSM_HEAD_DIM).reshape(1, -1), norm_g.reshape(1, -1), expand)


def _merge_kernel(h_ref, ya_ref, ys_ref, u_ref, vg_ref, gates_ref, lng_ref, lnb_ref, sw_ref,
                  sbt_ref, pa_ref, ps_ref, pc_ref, wo_ref, post_ref, o_ref, yc_ref):
    tm, d = h_ref.shape
    ch, gd = SGU_CHUNK, SGU_GROUP_DIM

    vg = _gelu_tanh(vg_ref[...].astype(F32))
    mu = jnp.mean(vg, axis=-1, keepdims=True)
    var = jnp.mean(jnp.square(vg - mu), axis=-1, keepdims=True)
    vn = ((vg - mu) * lax.rsqrt(var + NORM_EPS) * lng_ref[...] + lnb_ref[...]).astype(BF16)

    row = lax.broadcasted_iota(jnp.int32, (ch, ch), 0)
    col = lax.broadcasted_iota(jnp.int32, (ch, ch), 1)
    for g in range(SGU_GROUPS):
        w_sp = jnp.where(row >= col, sw_ref[g], 0.0).astype(BF16)
        bias = sbt_ref[:, g:g + 1]
        for c in range(tm // ch):
            rs, cs = slice(c * ch, (c + 1) * ch), slice(g * gd, (g + 1) * gd)
            sv = _dot(w_sp, vn[rs, cs]) + bias
            yc_ref[rs, cs] = (_gelu_tanh(u_ref[rs, cs].astype(F32)) * sv).astype(BF16)

    gates = gates_ref[...].astype(F32)
    merged = _sigmoid(gates[:, 0:d]) * _dot(ya_ref[...], pa_ref[...])
    merged += _sigmoid(gates[:, d:2 * d]) * _dot(ys_ref[...], ps_ref[...])
    merged += _sigmoid(gates[:, 2 * d:3 * d]) * _dot(yc_ref[...], pc_ref[...])
    m = _dot(merged.astype(BF16), wo_ref[...])
    o_ref[...] = h_ref[...] + _rms(m, post_ref[...])


def _merge(h, ya, ys, u, vg, gates, ln_g, ln_b, sgu_w, sgu_b, p_attn, p_ssm, p_sgu, w_out, post_g):
    t, d = h.shape
    tm = MERGE_TOKENS
    sw = SGU_GROUPS * SGU_GROUP_DIM
    assert t % tm == 0 and tm % SGU_CHUNK == 0
    tile = lambda w: pl.BlockSpec((tm, w), lambda i: (i, 0))
    return pl.pallas_call(
        _merge_kernel,
        out_shape=jax.ShapeDtypeStruct((t, d), F32),
        grid=(t // tm,),
        in_specs=[
            tile(d), tile(ya.shape[1]), tile(ys.shape[1]), tile(sw), tile(sw), tile(N_BRANCHES * d),
            _resident((1, sw)), _resident((1, sw)),
            _resident(sgu_w.shape), _resident((SGU_CHUNK, SGU_GROUPS)),
            _resident(p_attn.shape), _resident(p_ssm.shape), _resident(p_sgu.shape),
            _resident(w_out.shape), _resident((1, d)),
        ],
        out_specs=tile(d),
        scratch_shapes=[pltpu.VMEM((tm, sw), BF16)],
        compiler_params=pltpu.CompilerParams(
            dimension_semantics=("arbitrary",), vmem_limit_bytes=VMEM_LIMIT),
        name="merge",
    )(h, ya, ys, u, vg, gates, ln_g.reshape(1, -1), ln_b.reshape(1, -1), sgu_w, sgu_b.T,
      p_attn, p_ssm, p_sgu, w_out, post_g.reshape(1, d))


def kernel(x, ffn1_pre_g, ffn1_w_gu, ffn1_w_down, ffn1_post_g, mix_pre_g, w_in, conv_w, conv_b, dt_bias, a_log, d_skip, ssm_norm_g, sgu_ln_g, sgu_ln_b, sgu_w, sgu_b, p_attn, p_ssm, p_sgu, w_out, mix_post_g, ffn2_pre_g, ffn2_w_gu, ffn2_w_down, ffn2_post_g):
    b, s, d = x.shape
    depth = w_in.shape[0]
    h = x.reshape(b * s, d)
    as_b = lambda a: a.astype(BF16)
    for i in range(depth):
        h = _ffn(h, ffn1_pre_g[i], as_b(ffn1_w_gu[i]), as_b(ffn1_w_down[i]), ffn1_post_g[i])
        q, k, v, z, xbc, dt_raw, u, vg, gates = _inproj(h, mix_pre_g[i], _pad_w_in(w_in[i]))
        seq = lambda a: a.reshape(b, s, a.shape[-1])
        ya = _attention(seq(q), seq(k), seq(v))
        ys = _ssm(seq(xbc), seq(dt_raw), seq(z), conv_w[i], conv_b[i], dt_bias[i], a_log[i],
                  d_skip[i], ssm_norm_g[i])
        h = _merge(h, ya.reshape(b * s, -1), ys.reshape(b * s, -1), u, vg, gates,
                   sgu_ln_g[i], sgu_ln_b[i], sgu_w[i], sgu_b[i], as_b(p_attn[i]), as_b(p_ssm[i]),
                   as_b(p_sgu[i]), as_b(w_out[i]), mix_post_g[i])
        h = _ffn(h, ffn2_pre_g[i], as_b(ffn2_w_gu[i]), as_b(ffn2_w_down[i]), ffn2_post_g[i])
    return h.reshape(b, s, d)
```

```python
import functools

import jax
import jax.numpy as jnp
from jax import lax
from jax.experimental import pallas as pl
from jax.experimental.pallas import tpu as pltpu

F32 = jnp.float32
BF16 = jnp.bfloat16

NORM_EPS = 1e-6
ATTN_HEADS = 8
ATTN_HEAD_DIM = 128
MOBA_BLOCK = 256
MOBA_TOPK = 3
SSM_HEADS = 16
SSM_HEAD_DIM = 64
SSM_WIDTH = SSM_HEADS * SSM_HEAD_DIM
SSM_GROUPS = 2
SSM_STATE = 128
SSM_CONV = 4
SSM_CHUNK = 128
SSM_BC = SSM_GROUPS * SSM_STATE
SSM_CONV_CH = SSM_WIDTH + 2 * SSM_BC
SGU_GROUPS = 8
SGU_GROUP_DIM = 128
SGU_CHUNK = 128
N_BRANCHES = 3

LANES = 128
SUBLANES = 8
BF16_ROWS = 16
LOG2E = 1.4426950408889634
VMEM_LIMIT = 56 * 1024 * 1024
NEG_BIG = -1e30
POS_BIG = 1e30

FFN_TOKENS = 512
FFN_HIDDEN_CHUNK = 256
INPROJ_TOKENS = 256
INPROJ_COL_CHUNK = 512
SSM_TOKENS = 512
MERGE_TOKENS = 256
ATTN_GROUP = 8


def _resident(shape):
    nd = len(shape)
    return pl.BlockSpec(shape, lambda *_: (0,) * nd, pipeline_mode=pl.Buffered(1))


def _rms(x, g):
    return x * lax.rsqrt(jnp.mean(x * x, axis=-1, keepdims=True) + NORM_EPS) * g


def _sigmoid(x):
    return 1.0 / (1.0 + jnp.exp(-x))


def _gelu_tanh(x):
    return x * (0.5 * (1.0 + jnp.tanh(0.7978845608028654 * (x + 0.044715 * (x * x * x)))))


def _dot(a, b):
    return jnp.dot(a, b, preferred_element_type=F32)


def _dot_nt(a, b):
    return lax.dot_general(a, b, (((1,), (1,)), ((), ())), preferred_element_type=F32)


def _dot_tn(a, b):
    return lax.dot_general(a, b, (((0,), (0,)), ((), ())), preferred_element_type=F32)


def _split2(x):
    hi = x.astype(BF16)
    lo = (x - hi.astype(F32)).astype(BF16)
    return hi, lo


def _split3(x):
    hi = x.astype(BF16)
    r = x - hi.astype(F32)
    mid = r.astype(BF16)
    lo = (r - mid.astype(F32)).astype(BF16)
    return hi, mid, lo


def _ffn_kernel(h_ref, pre_ref, wgu_ref, wd_ref, post_ref, o_ref, acc_ref):
    x = h_ref[...]
    xn = _rms(x, pre_ref[...]).astype(BF16)
    hidden = wd_ref.shape[0]
    fc = FFN_HIDDEN_CHUNK
    for c in range(hidden // fc):
        g = _dot(xn, wgu_ref[:, c * fc:(c + 1) * fc])
        u = _dot(xn, wgu_ref[:, hidden + c * fc:hidden + (c + 1) * fc])
        a = (g * _sigmoid(g) * u).astype(BF16)
        d = _dot(a, wd_ref[c * fc:(c + 1) * fc, :])
        if c == 0:
            acc_ref[...] = d
        else:
            acc_ref[...] += d
    o_ref[...] = x + 0.5 * _rms(acc_ref[...], post_ref[...])


def _ffn(h, pre_g, w_gu, w_down, post_g):
    t, d = h.shape
    hidden = w_down.shape[0]
    assert hidden % FFN_HIDDEN_CHUNK == 0 and t % FFN_TOKENS == 0
    tm = FFN_TOKENS
    return pl.pallas_call(
        _ffn_kernel,
        out_shape=jax.ShapeDtypeStruct((t, d), F32),
        grid=(t // tm,),
        in_specs=[
            pl.BlockSpec((tm, d), lambda i: (i, 0)),
            _resident((1, d)),
            _resident((d, 2 * hidden)),
            _resident((hidden, d)),
            _resident((1, d)),
        ],
        out_specs=pl.BlockSpec((tm, d), lambda i: (i, 0)),
        scratch_shapes=[pltpu.VMEM((tm, d), F32)],
        compiler_params=pltpu.CompilerParams(
            dimension_semantics=("arbitrary",), vmem_limit_bytes=VMEM_LIMIT),
        name="ffn",
    )(h, pre_g.reshape(1, d), w_gu, w_down, post_g.reshape(1, d))


def _inproj_layout(d_model):
    aw = ATTN_HEADS * ATTN_HEAD_DIM
    sw = SGU_GROUPS * SGU_GROUP_DIM
    return (("q", aw, aw), ("k", aw, aw), ("v", aw, aw), ("z", SSM_WIDTH, SSM_WIDTH),
            ("xbc", SSM_CONV_CH, SSM_CONV_CH), ("dt", SSM_HEADS, LANES),
            ("u", sw, sw), ("vg", sw, sw), ("gates", N_BRANCHES * d_model, N_BRANCHES * d_model))


def _inproj_kernel(h_ref, pre_ref, w_ref, q_ref, k_ref, v_ref, z_ref, xbc_ref, dt_ref,
                   u_ref, vg_ref, gates_ref):
    xn = _rms(h_ref[...], pre_ref[...]).astype(BF16)
    outs = (q_ref, k_ref, v_ref, z_ref, xbc_ref, dt_ref, u_ref, vg_ref, gates_ref)
    off = 0
    for idx, o_ref in enumerate(outs):
        width = o_ref.shape[1]
        for c0 in range(0, width, INPROJ_COL_CHUNK):
            n = min(INPROJ_COL_CHUNK, width - c0)
            r = _dot(xn, w_ref[:, off + c0:off + c0 + n])
            if idx == 0:
                r = r * (ATTN_HEAD_DIM ** -0.5 * LOG2E)
            o_ref[:, c0:c0 + n] = r.astype(o_ref.dtype)
        off += width


def _inproj(h, pre_g, w_pad):
    t, d = h.shape
    tm = INPROJ_TOKENS
    layout = _inproj_layout(d)
    assert t % tm == 0 and w_pad.shape[1] == sum(p for _, _, p in layout)
    out_shape = tuple(jax.ShapeDtypeStruct((t, p), F32 if name == "dt" else BF16)
                      for name, _, p in layout)
    out_specs = tuple(pl.BlockSpec((tm, p), lambda i: (i, 0)) for _, _, p in layout)
    return pl.pallas_call(
        _inproj_kernel,
        out_shape=out_shape,
        grid=(t // tm,),
        in_specs=[
            pl.BlockSpec((tm, d), lambda i: (i, 0)),
            _resident((1, d)),
            _resident(w_pad.shape),
        ],
        out_specs=out_specs,
        compiler_params=pltpu.CompilerParams(
            dimension_semantics=("arbitrary",), vmem_limit_bytes=VMEM_LIMIT),
        name="inproj",
    )(h, pre_g.reshape(1, d), w_pad)


def _pad_w_in(w_in):
    d = w_in.shape[0]
    pieces, off = [], 0
    for _, width, padded in _inproj_layout(d):
        pieces.append(w_in[:, off:off + width])
        if padded > width:
            pieces.append(jnp.zeros((d, padded - width), w_in.dtype))
        off += width
    assert off == w_in.shape[1]
    return jnp.concatenate(pieces, axis=1).astype(BF16)


def _attn_kernel(slope_ref, q_ref, k_ref, v_ref, o_ref, vt_ref, kmean_ref, sel_ref, bias_ref,
                 s_ref, cmax_ref, p_ref, m_ref, acc_ref):
    blk = MOBA_BLOCK
    dh = ATTN_HEAD_DIM
    seq = q_ref.shape[1]
    nb = seq // blk
    gk = bias_ref.shape[0]
    group = gk // blk
    slope = slope_ref[0, 0:1, :] * LOG2E
    slope_row = jnp.concatenate([slope] * (blk // LANES), axis=1)
    bias_ref[...] = lax.broadcasted_iota(jnp.int32, (gk, blk), 0).astype(F32) * slope_row
    m_ref[...] = jnp.full(m_ref.shape, NEG_BIG, F32)
    acc_ref[...] = jnp.zeros_like(acc_ref)
    extra = lax.broadcasted_iota(jnp.int32, (vt_ref.shape[0] - dh, seq), 0)
    vt_ref[dh:, :] = jnp.where(extra == 0, 1.0, 0.0).astype(BF16)

    def prologue(jb, carry):
        off = pl.multiple_of(jb * blk, blk)
        kb = k_ref[0, pl.ds(off, blk), :].astype(F32)
        kmean_ref[pl.ds(jb, 1), :] = jnp.sum(kb, axis=0, keepdims=True) * (1.0 / blk)
        vb = v_ref[0, pl.ds(off, blk), :].astype(F32)
        vt_ref[0:dh, pl.ds(off, blk)] = vb.T.astype(BF16)
        return carry

    lax.fori_loop(0, nb, prologue, 0)

    km_hi, km_lo = _split2(kmean_ref[...])
    q_all = q_ref[0]
    gate = _dot_nt(km_hi, q_all) + _dot_nt(km_lo, q_all)
    blk_ids = lax.broadcasted_iota(jnp.int32, (nb, seq), 0)
    own_blk = lax.shift_right_logical(lax.broadcasted_iota(jnp.int32, (nb, seq), 1),
                                      blk.bit_length() - 1)
    gate = jnp.where(blk_ids < own_blk, gate, -jnp.inf)
    sel = jnp.where(blk_ids == own_blk, 1.0, 0.0)
    blk_ids = blk_ids.astype(F32)
    for _ in range(MOBA_TOPK):
        mx = jnp.max(gate, axis=0, keepdims=True)
        first = jnp.min(jnp.where(gate == mx, blk_ids, float(nb)), axis=0, keepdims=True)
        hit = blk_ids == first
        sel = jnp.where(jnp.logical_and(hit, mx > -jnp.inf), 1.0, sel)
        gate = jnp.where(hit, -jnp.inf, gate)
    sel_ref[...] = sel

    key_minus_qry = (lax.broadcasted_iota(jnp.int32, (blk, blk), 0)
                     - lax.broadcasted_iota(jnp.int32, (blk, blk), 1))

    def score_tile(g, step, slot, causal):
        r0 = pl.multiple_of(g * gk, gk)
        i = jnp.minimum(step, nb - 1)
        qi = q_ref[0, pl.ds(pl.multiple_of(i * blk, blk), blk), :]
        cms = []
        for u in range(group):
            rows = slice(u * blk, (u + 1) * blk)
            s = _dot_nt(k_ref[0, pl.ds(r0 + u * blk, blk), :], qi) + bias_ref[rows, :]
            if causal:
                s = jnp.where(key_minus_qry <= i * blk - g * gk - u * blk, s, NEG_BIG)
            s_ref[slot, rows, :] = s
            cms.append(jnp.max(s, axis=0, keepdims=True))
        cmax_ref[slot] = jnp.concatenate(cms, axis=0)

    def softmax_tile(g, i, slot):
        r0 = pl.multiple_of(g * gk, gk)
        off = slope_row * (g * gk - i * blk).astype(F32)
        chosen = sel_ref[pl.ds(pl.multiple_of(g * group, group), group),
                         pl.ds(pl.multiple_of(i * blk, blk), blk)] > 0.0
        m_old = m_ref[pl.ds(i, 1), :]
        m_new = jnp.maximum(m_old, jnp.max(jnp.where(chosen, cmax_ref[slot] + off, NEG_BIG),
                                           axis=0, keepdims=True))
        base = m_new - off
        for u in range(group):
            shift = jnp.where(chosen[u:u + 1, :], base, POS_BIG)
            pu = jnp.exp2(s_ref[slot, u * blk:(u + 1) * blk, :] - shift)
            p_ref[u * blk:(u + 1) * blk, :] = pu.astype(BF16)
        pv = _dot(vt_ref[:, pl.ds(r0, gk)], p_ref[...])
        acc_ref[i] = jnp.exp2(m_old - m_new) * acc_ref[i] + pv
        m_ref[pl.ds(i, 1), :] = m_new

    def key_group(g, carry):
        first = g * group
        score_tile(g, first, 0, True)

        def tile_pair(t, causal):
            i = first + 2 * t
            score_tile(g, i + 1, 1, causal)
            softmax_tile(g, i, 0)
            score_tile(g, i + 2, 0, causal)
            softmax_tile(g, i + 1, 1)

        def diagonal_tiles(t, c):
            tile_pair(t, True)
            return c

        def later_tiles(t, c):
            tile_pair(t, False)
            return c

        lax.fori_loop(0, group // 2, diagonal_tiles, 0)
        lax.fori_loop(group // 2, (nb - first) // 2, later_tiles, 0)
        return carry

    lax.fori_loop(0, nb // group, key_group, 0)

    def finalize(i, carry):
        a = acc_ref[i]
        out = a[0:dh, :] / a[dh:dh + 1, :]
        o_ref[0, pl.ds(pl.multiple_of(i * blk, blk), blk), :] = out.T.astype(o_ref.dtype)
        return carry

    lax.fori_loop(0, nb, finalize, 0)


def _attention(q, k, v):
    b, s, w = q.shape
    dh = ATTN_HEAD_DIM
    nh = w // dh
    assert s % MOBA_BLOCK == 0 and s // MOBA_BLOCK >= MOBA_TOPK and dh == LANES
    nb = s // MOBA_BLOCK
    group = min(ATTN_GROUP, nb)
    assert nb % group == 0
    gk = group * MOBA_BLOCK
    slopes = 2.0 **(-8.0 * (jnp.arange(nh, dtype=F32) + 1.0) / nh)
    slopes = jnp.broadcast_to(slopes[:, None, None], (nh, SUBLANES, LANES))
    head_spec = pl.BlockSpec((1, s, dh), lambda bi, hi: (bi, 0, hi))
    return pl.pallas_call(
        _attn_kernel,
        out_shape=jax.ShapeDtypeStruct((b, s, w), BF16),
        grid=(b, nh),
        in_specs=[pl.BlockSpec((1, SUBLANES, LANES), lambda bi, hi: (hi, 0, 0)),
                  head_spec, head_spec, head_spec],
        out_specs=head_spec,
        scratch_shapes=[
            pltpu.VMEM((dh + BF16_ROWS, s), BF16),
            pltpu.VMEM((nb, dh), F32),
            pltpu.VMEM((nb, s), F32),
            pltpu.VMEM((gk, MOBA_BLOCK), F32),
            pltpu.VMEM((2, gk, MOBA_BLOCK), F32),
            pltpu.VMEM((2, group, MOBA_BLOCK), F32),
            pltpu.VMEM((gk, MOBA_BLOCK), BF16),
            pltpu.VMEM((nb, MOBA_BLOCK), F32),
            pltpu.VMEM((nb, dh + BF16_ROWS, MOBA_BLOCK), F32),
        ],
        compiler_params=pltpu.CompilerParams(
            dimension_semantics=("arbitrary", "arbitrary"), vmem_limit_bytes=VMEM_LIMIT),
        name="moba_attention",
    )(slopes, q, k, v)


def _ssm_kernel(xbc_ref, dt_ref, z_ref, convw_ref, convb_ref, dtb_ref, alog_ref, dskip_ref,
                normg_ref, expand_ref, o_ref, xraw_ref, xc_ref, state_ref, y_ref):
    ts = xbc_ref.shape[1]
    q = SSM_CHUNK
    hp = SSM_HEAD_DIM
    gw = SSM_WIDTH // SSM_GROUPS
    hpg = SSM_HEADS // SSM_GROUPS
    pad = SUBLANES

    @pl.when(pl.program_id(1) == 0)
    def _():
        xraw_ref[0:pad, :] = jnp.zeros((pad, SSM_CONV_CH), F32)
        state_ref[...] = jnp.zeros_like(state_ref)

    xraw_ref[pad:pad + ts, :] = xbc_ref[0].astype(F32)
    conv = convb_ref[...] + convw_ref[SSM_CONV - 1:SSM_CONV, :] * xraw_ref[pad:pad + ts, :]
    for kk in range(SSM_CONV - 1):
        lag = SSM_CONV - 1 - kk
        conv = conv + convw_ref[kk:kk + 1, :] * xraw_ref[pad - lag:pad - lag + ts, :]
    xc_ref[...] = conv * _sigmoid(conv)
    xraw_ref[0:pad, :] = xraw_ref[ts:ts + pad, :]

    a_neg = -jnp.exp(alog_ref[...])
    expand = expand_ref[...]
    row = lax.broadcasted_iota(jnp.int32, (q, q), 0)
    col = lax.broadcasted_iota(jnp.int32, (q, q), 1)
    lower = row >= col
    tri = jnp.where(lower, 1.0, 0.0).astype(BF16)

    def expand_heads(w):
        hi, lo = _split2(w)
        return _dot(hi, expand) + _dot(lo, expand)

    def chunk(c, carry):
        r0 = pl.multiple_of(c * q, q)
        rows = pl.ds(r0, q)
        xs = xc_ref[rows, 0:SSM_WIDTH]
        xs_b = xs.astype(BF16)
        dtr = dt_ref[0, rows, :] + dtb_ref[...]
        dt = jnp.maximum(dtr, 0.0) + jnp.log1p(jnp.exp(-jnp.abs(dtr)))
        da = dt * a_neg
        d_hi, d_mid, d_lo = _split3(da)
        a_cs = _dot(tri, d_hi) + _dot(tri, d_mid) + _dot(tri, d_lo)
        a_cs_t = a_cs.T
        dt_t = dt.T
        a_last = a_cs[q - 1:q, :]

        ea_x = expand_heads(jnp.exp(a_cs))
        wend_x = expand_heads(jnp.exp(a_last - a_cs) * dt)
        cdec_x = ea_x[q - 1:q, :]
        xs_w = (xs * wend_x).astype(BF16)

        for g in range(SSM_GROUPS):
            bg = xc_ref[rows, SSM_WIDTH + g * SSM_STATE:SSM_WIDTH + (g + 1) * SSM_STATE].astype(BF16)
            cg = xc_ref[rows, SSM_WIDTH + SSM_BC + g * SSM_STATE:
                        SSM_WIDTH + SSM_BC + (g + 1) * SSM_STATE].astype(BF16)
            cb = _dot_nt(cg, bg)
            lanes = slice(g * gw, (g + 1) * gw)
            st = state_ref[:, lanes]
            y_ref[:, lanes] = _dot(cg, st.astype(BF16)) * ea_x[:, lanes]
            state_ref[:, lanes] = st * cdec_x[:, lanes] + _dot_tn(bg, xs_w[:, lanes])
            for hh in range(0, hpg, 2):
                pair = []
                for h in (g * hpg + hh, g * hpg + hh + 1):
                    seg = a_cs[:, h:h + 1] - a_cs_t[h:h + 1, :]
                    wts = cb * jnp.where(lower, jnp.exp(seg), 0.0) * dt_t[h:h + 1, :]
                    pair.append(_dot(wts.astype(BF16), xs_b[:, h * hp:(h + 1) * hp]))
                h0 = (g * hpg + hh) * hp
                y_ref[:, h0:h0 + 2 * hp] += jnp.concatenate(pair, axis=1)

        y = y_ref[...] + dskip_ref[...] * xs
        zz = z_ref[0, rows, :].astype(F32)
        y = y * (zz * _sigmoid(zz))
        o_ref[0, rows, :] = _rms(y, normg_ref[...]).astype(o_ref.dtype)
        return carry

    lax.fori_loop(0, ts // q, chunk, 0)


def _ssm(xbc, dt_raw, z, conv_w, conv_b, dt_bias, a_log, d_skip, norm_g):
    b, s, _ = xbc.shape
    ts = min(SSM_TOKENS, s)
    assert s % ts == 0 and ts % SSM_CHUNK == 0 and SSM_HEADS <= LANES

    def lane_pad(x):
        return jnp.zeros((1, LANES), F32).at[0, :SSM_HEADS].set(x)

    head_of_lane = jnp.arange(SSM_WIDTH) // SSM_HEAD_DIM
    expand = (jnp.arange(LANES)[:, None] == head_of_lane[None, :]).astype(BF16)
    tile = lambda w: pl.BlockSpec((1, ts, w), lambda bi, ti: (bi, ti, 0))
    return pl.pallas_call(
        _ssm_kernel,
        out_shape=jax.ShapeDtypeStruct((b, s, SSM_WIDTH), BF16),
        grid=(b, s // ts),
        in_specs=[
            tile(SSM_CONV_CH), tile(LANES), tile(SSM_WIDTH),
            _resident((SSM_CONV, SSM_CONV_CH)), _resident((1, SSM_CONV_CH)),
            _resident((1, LANES)), _resident((1, LANES)),
            _resident((1, SSM_WIDTH)), _resident((1, SSM_WIDTH)),
            _resident((LANES, SSM_WIDTH)),
        ],
        out_specs=tile(SSM_WIDTH),
        scratch_shapes=[
            pltpu.VMEM((ts + SUBLANES, SSM_CONV_CH), F32),
            pltpu.VMEM((ts, SSM_CONV_CH), F32),
            pltpu.VMEM((SSM_STATE, SSM_WIDTH), F32),
            pltpu.VMEM((SSM_CHUNK, SSM_WIDTH), F32),
        ],
        compiler_params=pltpu.CompilerParams(
            dimension_semantics=("arbitrary", "arbitrary"), vmem_limit_bytes=VMEM_LIMIT),
        name="ssd",
    )(xbc, dt_raw, z, conv_w, conv_b.reshape(1, -1), lane_pad(dt_bias), lane_pad(a_log),
      jnp.repeat(d_skip, SSM_HEAD_DIM).reshape(1, -1), norm_g.reshape(1, -1), expand)


def _merge_kernel(h_ref, ya_ref, ys_ref, u_ref, vg_ref, gates_ref, lng_ref, lnb_ref, sw_ref,
                  sbt_ref, pa_ref, ps_ref, pc_ref, wo_ref, post_ref, o_ref, yc_ref):
    tm, d = h_ref.shape
    ch, gd = SGU_CHUNK, SGU_GROUP_DIM

    vg = _gelu_tanh(vg_ref[...].astype(F32))
    mu = jnp.mean(vg, axis=-1, keepdims=True)
    var = jnp.mean(jnp.square(vg - mu), axis=-1, keepdims=True)
    vn = ((vg - mu) * lax.rsqrt(var + NORM_EPS) * lng_ref[...] + lnb_ref[...]).astype(BF16)

    row = lax.broadcasted_iota(jnp.int32, (ch, ch), 0)
    col = lax.broadcasted_iota(jnp.int32, (ch, ch), 1)
    for g in range(SGU_GROUPS):
        w_sp = jnp.where(row >= col, sw_ref[g], 0.0).astype(BF16)
        bias = sbt_ref[:, g:g + 1]
        for c in range(tm // ch):
            rs, cs = slice(c * ch, (c + 1) * ch), slice(g * gd, (g + 1) * gd)
            sv = _dot(w_sp, vn[rs, cs]) + bias
            yc_ref[rs, cs] = (_gelu_tanh(u_ref[rs, cs].astype(F32)) * sv).astype(BF16)

    gates = gates_ref[...].astype(F32)
    merged = _sigmoid(gates[:, 0:d]) * _dot(ya_ref[...], pa_ref[...])
    merged += _sigmoid(gates[:, d:2 * d]) * _dot(ys_ref[...], ps_ref[...])
    merged += _sigmoid(gates[:, 2 * d:3 * d]) * _dot(yc_ref[...], pc_ref[...])
    m = _dot(merged.astype(BF16), wo_ref[...])
    o_ref[...] = h_ref[...] + _rms(m, post_ref[...])


def _merge(h, ya, ys, u, vg, gates, ln_g, ln_b, sgu_w, sgu_b, p_attn, p_ssm, p_sgu, w_out, post_g):
    t, d = h.shape
    tm = MERGE_TOKENS
    sw = SGU_GROUPS * SGU_GROUP_DIM
    assert t % tm == 0 and tm % SGU_CHUNK == 0
    tile = lambda w: pl.BlockSpec((tm, w), lambda i: (i, 0))
    return pl.pallas_call(
        _merge_kernel,
        out_shape=jax.ShapeDtypeStruct((t, d), F32),
        grid=(t // tm,),
        in_specs=[
            tile(d), tile(ya.shape[1]), tile(ys.shape[1]), tile(sw), tile(sw), tile(N_BRANCHES * d),
            _resident((1, sw)), _resident((1, sw)),
            _resident(sgu_w.shape), _resident((SGU_CHUNK, SGU_GROUPS)),
            _resident(p_attn.shape), _resident(p_ssm.shape), _resident(p_sgu.shape),
            _resident(w_out.shape), _resident((1, d)),
        ],
        out_specs=tile(d),
        scratch_shapes=[pltpu.VMEM((tm, sw), BF16)],
        compiler_params=pltpu.CompilerParams(
            dimension_semantics=("arbitrary",), vmem_limit_bytes=VMEM_LIMIT),
        name="merge",
    )(h, ya, ys, u, vg, gates, ln_g.reshape(1, -1), ln_b.reshape(1, -1), sgu_w, sgu_b.T,
      p_attn, p_ssm, p_sgu, w_out, post_g.reshape(1, d))


def kernel(x, ffn1_pre_g, ffn1_w_gu, ffn1_w_down, ffn1_post_g, mix_pre_g, w_in, conv_w, conv_b, dt_bias, a_log, d_skip, ssm_norm_g, sgu_ln_g, sgu_ln_b, sgu_w, sgu_b, p_attn, p_ssm, p_sgu, w_out, mix_post_g, ffn2_pre_g, ffn2_w_gu, ffn2_w_down, ffn2_post_g):
    b, s, d = x.shape
    depth = w_in.shape[0]
    h = x.reshape(b * s, d)
    as_b = lambda a: a.astype(BF16)
    for i in range(depth):
        h = _ffn(h, ffn1_pre_g[i], as_b(ffn1_w_gu[i]), as_b(ffn1_w_down[i]), ffn1_post_g[i])
        q, k, v, z, xbc, dt_raw, u, vg, gates = _inproj(h, mix_pre_g[i], _pad_w_in(w_in[i]))
        seq = lambda a: a.reshape(b, s, a.shape[-1])
        ya = _attention(seq(q), seq(k), seq(v))
        ys = _ssm(seq(xbc), seq(dt_raw), seq(z), conv_w[i], conv_b[i], dt_bias[i], a_log[i],
                  d_skip[i], ssm_norm_g[i])
        h = _merge(h, ya.reshape(b * s, -1), ys.reshape(b * s, -1), u, vg, gates,
                   sgu_ln_g[i], sgu_ln_b[i], sgu_w[i], sgu_b[i], as_b(p_attn[i]), as_b(p_ssm[i]),
                   as_b(p_sgu[i]), as_b(w_out[i]), mix_post_g[i])
        h = _ffn(h, ffn2_pre_g[i], as_b(ffn2_w_gu[i]), as_b(ffn2_w_down[i]), ffn2_post_g[i])
    return h.reshape(b, s, d)
```

```python
import functools

import jax
import jax.numpy as jnp
from jax import lax
from jax.experimental import pallas as pl
from jax.experimental.pallas import tpu as pltpu

F32 = jnp.float32
BF16 = jnp.bfloat16

NORM_EPS = 1e-6
ATTN_HEADS = 8
ATTN_HEAD_DIM = 128
MOBA_BLOCK = 256
MOBA_TOPK = 3
SSM_HEADS = 16
SSM_HEAD_DIM = 64
SSM_WIDTH = SSM_HEADS * SSM_HEAD_DIM
SSM_GROUPS = 2
SSM_STATE = 128
SSM_CONV = 4
SSM_CHUNK = 128
SSM_BC = SSM_GROUPS * SSM_STATE
SSM_CONV_CH = SSM_WIDTH + 2 * SSM_BC
SGU_GROUPS = 8
SGU_GROUP_DIM = 128
SGU_CHUNK = 128
N_BRANCHES = 3

LANES = 128
SUBLANES = 8
BF16_ROWS = 16
LOG2E = 1.4426950408889634
VMEM_LIMIT = 56 * 1024 * 1024
NEG_BIG = -1e30
POS_BIG = 1e30

FFN_TOKENS = 512
FFN_HIDDEN_CHUNK = 256
INPROJ_TOKENS = 256
INPROJ_COL_CHUNK = 512
SSM_TOKENS = 512
MERGE_TOKENS = 256
ATTN_GROUP = 8


def _resident(shape):
    nd = len(shape)
    return pl.BlockSpec(shape, lambda *_: (0,) * nd, pipeline_mode=pl.Buffered(1))


def _rms(x, g):
    return x * lax.rsqrt(jnp.mean(x * x, axis=-1, keepdims=True) + NORM_EPS) * g


def _sigmoid(x):
    return 1.0 / (1.0 + jnp.exp(-x))


def _gelu_tanh(x):
    return x * (0.5 * (1.0 + jnp.tanh(0.7978845608028654 * (x + 0.044715 * (x * x * x)))))


def _dot(a, b):
    return jnp.dot(a, b, preferred_element_type=F32)


def _dot_nt(a, b):
    return lax.dot_general(a, b, (((1,), (1,)), ((), ())), preferred_element_type=F32)


def _dot_tn(a, b):
    return lax.dot_general(a, b, (((0,), (0,)), ((), ())), preferred_element_type=F32)


def _split2(x):
    hi = x.astype(BF16)
    lo = (x - hi.astype(F32)).astype(BF16)
    return hi, lo


def _split3(x):
    hi = x.astype(BF16)
    r = x - hi.astype(F32)
    mid = r.astype(BF16)
    lo = (r - mid.astype(F32)).astype(BF16)
    return hi, mid, lo


def _ffn_kernel(h_ref, pre_ref, wgu_ref, wd_ref, post_ref, o_ref, acc_ref):
    x = h_ref[...]
    xn = _rms(x, pre_ref[...]).astype(BF16)
    hidden = wd_ref.shape[0]
    fc = FFN_HIDDEN_CHUNK
    for c in range(hidden // fc):
        g = _dot(xn, wgu_ref[:, c * fc:(c + 1) * fc])
        u = _dot(xn, wgu_ref[:, hidden + c * fc:hidden + (c + 1) * fc])
        a = (g * _sigmoid(g) * u).astype(BF16)
        d = _dot(a, wd_ref[c * fc:(c + 1) * fc, :])
        if c == 0:
            acc_ref[...] = d
        else:
            acc_ref[...] += d
    o_ref[...] = x + 0.5 * _rms(acc_ref[...], post_ref[...])


def _ffn(h, pre_g, w_gu, w_down, post_g):
    t, d = h.shape
    hidden = w_down.shape[0]
    assert hidden % FFN_HIDDEN_CHUNK == 0 and t % FFN_TOKENS == 0
    tm = FFN_TOKENS
    return pl.pallas_call(
        _ffn_kernel,
        out_shape=jax.ShapeDtypeStruct((t, d), F32),
        grid=(t // tm,),
        in_specs=[
            pl.BlockSpec((tm, d), lambda i: (i, 0)),
            _resident((1, d)),
            _resident((d, 2 * hidden)),
            _resident((hidden, d)),
            _resident((1, d)),
        ],
        out_specs=pl.BlockSpec((tm, d), lambda i: (i, 0)),
        scratch_shapes=[pltpu.VMEM((tm, d), F32)],
        compiler_params=pltpu.CompilerParams(
            dimension_semantics=("arbitrary",), vmem_limit_bytes=VMEM_LIMIT),
        name="ffn",
    )(h, pre_g.reshape(1, d), w_gu, w_down, post_g.reshape(1, d))


def _inproj_layout(d_model):
    aw = ATTN_HEADS * ATTN_HEAD_DIM
    sw = SGU_GROUPS * SGU_GROUP_DIM
    return (("q", aw, aw), ("k", aw, aw), ("v", aw, aw), ("z", SSM_WIDTH, SSM_WIDTH),
            ("xbc", SSM_CONV_CH, SSM_CONV_CH), ("dt", SSM_HEADS, LANES),
            ("u", sw, sw), ("vg", sw, sw), ("gates", N_BRANCHES * d_model, N_BRANCHES * d_model))


def _inproj_kernel(h_ref, pre_ref, w_ref, q_ref, k_ref, v_ref, z_ref, xbc_ref, dt_ref,
                   u_ref, vg_ref, gates_ref):
    xn = _rms(h_ref[...], pre_ref[...]).astype(BF16)
    outs = (q_ref, k_ref, v_ref, z_ref, xbc_ref, dt_ref, u_ref, vg_ref, gates_ref)
    off = 0
    for idx, o_ref in enumerate(outs):
        width = o_ref.shape[1]
        for c0 in range(0, width, INPROJ_COL_CHUNK):
            n = min(INPROJ_COL_CHUNK, width - c0)
            r = _dot(xn, w_ref[:, off + c0:off + c0 + n])
            if idx == 0:
                r = r * (ATTN_HEAD_DIM ** -0.5 * LOG2E)
            o_ref[:, c0:c0 + n] = r.astype(o_ref.dtype)
        off += width


def _inproj(h, pre_g, w_pad):
    t, d = h.shape
    tm = INPROJ_TOKENS
    layout = _inproj_layout(d)
    assert t % tm == 0 and w_pad.shape[1] == sum(p for _, _, p in layout)
    out_shape = tuple(jax.ShapeDtypeStruct((t, p), F32 if name == "dt" else BF16)
                      for name, _, p in layout)
    out_specs = tuple(pl.BlockSpec((tm, p), lambda i: (i, 0)) for _, _, p in layout)
    return pl.pallas_call(
        _inproj_kernel,
        out_shape=out_shape,
        grid=(t // tm,),
        in_specs=[
            pl.BlockSpec((tm, d), lambda i: (i, 0)),
            _resident((1, d)),
            _resident(w_pad.shape),
        ],
        out_specs=out_specs,
        compiler_params=pltpu.CompilerParams(
            dimension_semantics=("arbitrary",), vmem_limit_bytes=VMEM_LIMIT),
        name="inproj",
    )(h, pre_g.reshape(1, d), w_pad)


def _pad_w_in(w_in):
    d = w_in.shape[0]
    pieces, off = [], 0
    for _, width, padded in _inproj_layout(d):
        pieces.append(w_in[:, off:off + width])
        if padded > width:
            pieces.append(jnp.zeros((d, padded - width), w_in.dtype))
        off += width
    assert off == w_in.shape[1]
    return jnp.concatenate(pieces, axis=1).astype(BF16)


def _attn_kernel(slope_ref, q_ref, k_ref, v_ref, o_ref, vt_ref, kmean_ref, sel_ref, bias_ref,
                 s_ref, cmax_ref, p_ref, m_ref, acc_ref):
    blk = MOBA_BLOCK
    dh = ATTN_HEAD_DIM
    seq = q_ref.shape[1]
    nb = seq // blk
    gk = bias_ref.shape[0]
    group = gk // blk
    slope = slope_ref[0, 0:1, :] * LOG2E
    slope_row = jnp.concatenate([slope] * (blk // LANES), axis=1)
    bias_ref[...] = lax.broadcasted_iota(jnp.int32, (gk, blk), 0).astype(F32) * slope_row
    m_ref[...] = jnp.full(m_ref.shape, NEG_BIG, F32)
    acc_ref[...] = jnp.zeros_like(acc_ref)
    extra = lax.broadcasted_iota(jnp.int32, (vt_ref.shape[0] - dh, seq), 0)
    vt_ref[dh:, :] = jnp.where(extra == 0, 1.0, 0.0).astype(BF16)

    def prologue(jb, carry):
        off = pl.multiple_of(jb * blk, blk)
        kb = k_ref[0, pl.ds(off, blk), :].astype(F32)
        kmean_ref[pl.ds(jb, 1), :] = jnp.sum(kb, axis=0, keepdims=True) * (1.0 / blk)
        vb = v_ref[0, pl.ds(off, blk), :].astype(F32)
        vt_ref[0:dh, pl.ds(off, blk)] = vb.T.astype(BF16)
        return carry

    lax.fori_loop(0, nb, prologue, 0, unroll=2)

    km_hi, km_lo = _split2(kmean_ref[...])
    q_all = q_ref[0]
    gate = _dot_nt(km_hi, q_all) + _dot_nt(km_lo, q_all)
    blk_ids = lax.broadcasted_iota(jnp.int32, (nb, seq), 0)
    own_blk = lax.shift_right_logical(lax.broadcasted_iota(jnp.int32, (nb, seq), 1),
                                      blk.bit_length() - 1)
    gate = jnp.where(blk_ids < own_blk, gate, -jnp.inf)
    sel = jnp.where(blk_ids == own_blk, 1.0, 0.0)
    blk_ids = blk_ids.astype(F32)
    for _ in range(MOBA_TOPK):
        mx = jnp.max(gate, axis=0, keepdims=True)
        first = jnp.min(jnp.where(gate == mx, blk_ids, float(nb)), axis=0, keepdims=True)
        hit = blk_ids == first
        sel = jnp.where(jnp.logical_and(hit, mx > -jnp.inf), 1.0, sel)
        gate = jnp.where(hit, -jnp.inf, gate)
    sel_ref[...] = sel

    key_minus_qry = (lax.broadcasted_iota(jnp.int32, (blk, blk), 0)
                     - lax.broadcasted_iota(jnp.int32, (blk, blk), 1))

    def score_tile(g, step, slot, nblocks=group, own_block=None):
        r0 = pl.multiple_of(g * gk, gk)
        i = jnp.minimum(step, nb - 1)
        qi = q_ref[0, pl.ds(pl.multiple_of(i * blk, blk), blk), :]
        cms = []
        for u in range(nblocks):
            rows = slice(u * blk, (u + 1) * blk)
            s = _dot_nt(k_ref[0, pl.ds(r0 + u * blk, blk), :], qi) + bias_ref[rows, :]
            if u == own_block:
                s = jnp.where(key_minus_qry <= 0, s, NEG_BIG)
            s_ref[slot, rows, :] = s
            cms.append(jnp.max(s, axis=0, keepdims=True))
        cms += [jnp.full((1, blk), NEG_BIG, F32)] * (group - nblocks)
        cmax_ref[slot] = jnp.concatenate(cms, axis=0)

    def softmax_tile(g, i, slot, nblocks=group):
        r0 = pl.multiple_of(g * gk, gk)
        off = slope_row * jnp.asarray(g * gk - i * blk, F32)
        chosen = sel_ref[pl.ds(pl.multiple_of(g * group, group), group),
                         pl.ds(pl.multiple_of(i * blk, blk), blk)] > 0.0
        m_old = m_ref[pl.ds(i, 1), :]
        m_new = jnp.maximum(m_old, jnp.max(jnp.where(chosen, cmax_ref[slot] + off, NEG_BIG),
                                           axis=0, keepdims=True))
        base = m_new - off
        for u in range(nblocks):
            shift = jnp.where(chosen[u:u + 1, :], base, POS_BIG)
            pu = jnp.exp2(s_ref[slot, u * blk:(u + 1) * blk, :] - shift)
            p_ref[u * blk:(u + 1) * blk, :] = pu.astype(BF16)
        pv = _dot(vt_ref[:, pl.ds(r0, nblocks * blk)], p_ref[0:nblocks * blk, :])
        acc_ref[i] = jnp.exp2(m_old - m_new) * acc_ref[i] + pv
        m_ref[pl.ds(i, 1), :] = m_new

    def key_group(g, carry):
        first = g * group
        score_tile(g, first, 0, 1, 0)
        for r in range(group):
            if r + 1 < group:
                score_tile(g, first + r + 1, (r + 1) % 2, r + 2, r + 1)
            else:
                score_tile(g, first + group, group % 2)
            softmax_tile(g, first + r, r % 2, r + 1)

        def later_tiles(t, c):
            i = first + 2 * t
            score_tile(g, i + 1, 1)
            softmax_tile(g, i, 0)
            score_tile(g, i + 2, 0)
            softmax_tile(g, i + 1, 1)
            return c

        lax.fori_loop(group // 2, (nb - first) // 2, later_tiles, 0)
        return carry

    lax.fori_loop(0, nb // group, key_group, 0)

    def finalize(i, carry):
        a = acc_ref[i]
        out = a[0:dh, :] / a[dh:dh + 1, :]
        o_ref[0, pl.ds(pl.multiple_of(i * blk, blk), blk), :] = out.T.astype(o_ref.dtype)
        return carry

    lax.fori_loop(0, nb, finalize, 0, unroll=2)


def _attention(q, k, v):
    b, s, w = q.shape
    dh = ATTN_HEAD_DIM
    nh = w // dh
    assert s % MOBA_BLOCK == 0 and s // MOBA_BLOCK >= MOBA_TOPK and dh == LANES
    nb = s // MOBA_BLOCK
    group = min(ATTN_GROUP, nb)
    assert nb % group == 0
    gk = group * MOBA_BLOCK
    slopes = 2.0 **(-8.0 * (jnp.arange(nh, dtype=F32) + 1.0) / nh)
    slopes = jnp.broadcast_to(slopes[:, None, None], (nh, SUBLANES, LANES))
    head_spec = pl.BlockSpec((1, s, dh), lambda bi, hi: (bi, 0, hi))
    return pl.pallas_call(
        _attn_kernel,
        out_shape=jax.ShapeDtypeStruct((b, s, w), BF16),
        grid=(b, nh),
        in_specs=[pl.BlockSpec((1, SUBLANES, LANES), lambda bi, hi: (hi, 0, 0)),
                  head_spec, head_spec, head_spec],
        out_specs=head_spec,
        scratch_shapes=[
            pltpu.VMEM((dh + BF16_ROWS, s), BF16),
            pltpu.VMEM((nb, dh), F32),
            pltpu.VMEM((nb, s), F32),
            pltpu.VMEM((gk, MOBA_BLOCK), F32),
            pltpu.VMEM((2, gk, MOBA_BLOCK), F32),
            pltpu.VMEM((2, group, MOBA_BLOCK), F32),
            pltpu.VMEM((gk, MOBA_BLOCK), BF16),
            pltpu.VMEM((nb, MOBA_BLOCK), F32),
            pltpu.VMEM((nb, dh + BF16_ROWS, MOBA_BLOCK), F32),
        ],
        compiler_params=pltpu.CompilerParams(
            dimension_semantics=("arbitrary", "arbitrary"), vmem_limit_bytes=VMEM_LIMIT),
        name="moba_attention",
    )(slopes, q, k, v)


def _ssm_kernel(xbc_ref, dt_ref, z_ref, convw_ref, convb_ref, dtb_ref, alog_ref, dskip_ref,
                normg_ref, expand_ref, o_ref, xraw_ref, xc_ref, state_ref, y_ref):
    ts = xbc_ref.shape[1]
    q = SSM_CHUNK
    hp = SSM_HEAD_DIM
    gw = SSM_WIDTH // SSM_GROUPS
    hpg = SSM_HEADS // SSM_GROUPS
    pad = SUBLANES

    @pl.when(pl.program_id(1) == 0)
    def _():
        xraw_ref[0:pad, :] = jnp.zeros((pad, SSM_CONV_CH), F32)
        state_ref[...] = jnp.zeros_like(state_ref)

    xraw_ref[pad:pad + ts, :] = xbc_ref[0].astype(F32)
    conv = convb_ref[...] + convw_ref[SSM_CONV - 1:SSM_CONV, :] * xraw_ref[pad:pad + ts, :]
    for kk in range(SSM_CONV - 1):
        lag = SSM_CONV - 1 - kk
        conv = conv + convw_ref[kk:kk + 1, :] * xraw_ref[pad - lag:pad - lag + ts, :]
    xc_ref[...] = conv * _sigmoid(conv)
    xraw_ref[0:pad, :] = xraw_ref[ts:ts + pad, :]

    a_neg = -jnp.exp(alog_ref[...])
    expand = expand_ref[...]
    row = lax.broadcasted_iota(jnp.int32, (q, q), 0)
    col = lax.broadcasted_iota(jnp.int32, (q, q), 1)
    lower = row >= col
    tri = jnp.where(lower, 1.0, 0.0).astype(BF16)

    def expand_heads(w):
        hi, lo = _split2(w)
        return _dot(hi, expand) + _dot(lo, expand)

    def chunk(c, carry):
        r0 = pl.multiple_of(c * q, q)
        rows = pl.ds(r0, q)
        xs = xc_ref[rows, 0:SSM_WIDTH]
        xs_b = xs.astype(BF16)
        dtr = dt_ref[0, rows, :] + dtb_ref[...]
        dt = jnp.maximum(dtr, 0.0) + jnp.log1p(jnp.exp(-jnp.abs(dtr)))
        da = dt * a_neg
        d_hi, d_mid, d_lo = _split3(da)
        a_cs = _dot(tri, d_hi) + _dot(tri, d_mid) + _dot(tri, d_lo)
        a_cs_t = a_cs.T
        dt_t = dt.T
        a_last = a_cs[q - 1:q, :]

        ea_x = expand_heads(jnp.exp(a_cs))
        wend_x = expand_heads(jnp.exp(a_last - a_cs) * dt)
        cdec_x = ea_x[q - 1:q, :]
        xs_w = (xs * wend_x).astype(BF16)

        for g in range(SSM_GROUPS):
            bg = xc_ref[rows, SSM_WIDTH + g * SSM_STATE:SSM_WIDTH + (g + 1) * SSM_STATE].astype(BF16)
            cg = xc_ref[rows, SSM_WIDTH + SSM_BC + g * SSM_STATE:
                        SSM_WIDTH + SSM_BC + (g + 1) * SSM_STATE].astype(BF16)
            cb = _dot_nt(cg, bg)
            lanes = slice(g * gw, (g + 1) * gw)
            st = state_ref[:, lanes]
            y_ref[:, lanes] = _dot(cg, st.astype(BF16)) * ea_x[:, lanes]
            state_ref[:, lanes] = st * cdec_x[:, lanes] + _dot_tn(bg, xs_w[:, lanes])
            for hh in range(0, hpg, 2):
                pair = []
                for h in (g * hpg + hh, g * hpg + hh + 1):
                    seg = a_cs[:, h:h + 1] - a_cs_t[h:h + 1, :]
                    wts = cb * jnp.where(lower, jnp.exp(seg), 0.0) * dt_t[h:h + 1, :]
                    pair.append(_dot(wts.astype(BF16), xs_b[:, h * hp:(h + 1) * hp]))
                h0 = (g * hpg + hh) * hp
                y_ref[:, h0:h0 + 2 * hp] += jnp.concatenate(pair, axis=1)

        y = y_ref[...] + dskip_ref[...] * xs
        zz = z_ref[0, rows, :].astype(F32)
        y = y * (zz * _sigmoid(zz))
        o_ref[0, rows, :] = _rms(y, normg_ref[...]).astype(o_ref.dtype)
        return carry

    lax.fori_loop(0, ts // q, chunk, 0)


def _ssm(xbc, dt_raw, z, conv_w, conv_b, dt_bias, a_log, d_skip, norm_g):
    b, s, _ = xbc.shape
    ts = min(SSM_TOKENS, s)
    assert s % ts == 0 and ts % SSM_CHUNK == 0 and SSM_HEADS <= LANES

    def lane_pad(x):
        return jnp.zeros((1, LANES), F32).at[0, :SSM_HEADS].set(x)

    head_of_lane = jnp.arange(SSM_WIDTH) // SSM_HEAD_DIM
    expand = (jnp.arange(LANES)[:, None] == head_of_lane[None, :]).astype(BF16)
    tile = lambda w: pl.BlockSpec((1, ts, w), lambda bi, ti: (bi, ti, 0))
    return pl.pallas_call(
        _ssm_kernel,
        out_shape=jax.ShapeDtypeStruct((b, s, SSM_WIDTH), BF16),
        grid=(b, s // ts),
        in_specs=[
            tile(SSM_CONV_CH), tile(LANES), tile(SSM_WIDTH),
            _resident((SSM_CONV, SSM_CONV_CH)), _resident((1, SSM_CONV_CH)),
            _resident((1, LANES)), _resident((1, LANES)),
            _resident((1, SSM_WIDTH)), _resident((1, SSM_WIDTH)),
            _resident((LANES, SSM_WIDTH)),
        ],
        out_specs=tile(SSM_WIDTH),
        scratch_shapes=[
            pltpu.VMEM((ts + SUBLANES, SSM_CONV_CH), F32),
            pltpu.VMEM((ts, SSM_CONV_CH), F32),
            pltpu.VMEM((SSM_STATE, SSM_WIDTH), F32),
            pltpu.VMEM((SSM_CHUNK, SSM_WIDTH), F32),
        ],
        compiler_params=pltpu.CompilerParams(
            dimension_semantics=("arbitrary", "arbitrary"), vmem_limit_bytes=VMEM_LIMIT),
        name="ssd",
    )(xbc, dt_raw, z, conv_w, conv_b.reshape(1, -1), lane_pad(dt_bias), lane_pad(a_log),
      jnp.repeat(d_skip, SSM_HEAD_DIM).reshape(1, -1), norm_g.reshape(1, -1), expand)


def _merge_kernel(h_ref, ya_ref, ys_ref, u_ref, vg_ref, gates_ref, lng_ref, lnb_ref, sw_ref,
                  sbt_ref, pa_ref, ps_ref, pc_ref, wo_ref, post_ref, o_ref, yc_ref):
    tm, d = h_ref.shape
    ch, gd = SGU_CHUNK, SGU_GROUP_DIM

    vg = _gelu_tanh(vg_ref[...].astype(F32))
    mu = jnp.mean(vg, axis=-1, keepdims=True)
    var = jnp.mean(jnp.square(vg - mu), axis=-1, keepdims=True)
    vn = ((vg - mu) * lax.rsqrt(var + NORM_EPS) * lng_ref[...] + lnb_ref[...]).astype(BF16)

    row = lax.broadcasted_iota(jnp.int32, (ch, ch), 0)
    col = lax.broadcasted_iota(jnp.int32, (ch, ch), 1)
    for g in range(SGU_GROUPS):
        w_sp = jnp.where(row >= col, sw_ref[g], 0.0).astype(BF16)
        bias = sbt_ref[:, g:g + 1]
        for c in range(tm // ch):
            rs, cs = slice(c * ch, (c + 1) * ch), slice(g * gd, (g + 1) * gd)
            sv = _dot(w_sp, vn[rs, cs]) + bias
            yc_ref[rs, cs] = (_gelu_tanh(u_ref[rs, cs].astype(F32)) * sv).astype(BF16)

    gates = gates_ref[...].astype(F32)
    merged = _sigmoid(gates[:, 0:d]) * _dot(ya_ref[...], pa_ref[...])
    merged += _sigmoid(gates[:, d:2 * d]) * _dot(ys_ref[...], ps_ref[...])
    merged += _sigmoid(gates[:, 2 * d:3 * d]) * _dot(yc_ref[...], pc_ref[...])
    m = _dot(merged.astype(BF16), wo_ref[...])
    o_ref[...] = h_ref[...] + _rms(m, post_ref[...])


def _merge(h, ya, ys, u, vg, gates, ln_g, ln_b, sgu_w, sgu_b, p_attn, p_ssm, p_sgu, w_out, post_g):
    t, d = h.shape
    tm = MERGE_TOKENS
    sw = SGU_GROUPS * SGU_GROUP_DIM
    assert t % tm == 0 and tm % SGU_CHUNK == 0
    tile = lambda w: pl.BlockSpec((tm, w), lambda i: (i, 0))
    return pl.pallas_call(
        _merge_kernel,
        out_shape=jax.ShapeDtypeStruct((t, d), F32),
        grid=(t // tm,),
        in_specs=[
            tile(d), tile(ya.shape[1]), tile(ys.shape[1]), tile(sw), tile(sw), tile(N_BRANCHES * d),
            _resident((1, sw)), _resident((1, sw)),
            _resident(sgu_w.shape), _resident((SGU_CHUNK, SGU_GROUPS)),
            _resident(p_attn.shape), _resident(p_ssm.shape), _resident(p_sgu.shape),
            _resident(w_out.shape), _resident((1, d)),
        ],
        out_specs=tile(d),
        scratch_shapes=[pltpu.VMEM((tm, sw), BF16)],
        compiler_params=pltpu.CompilerParams(
            dimension_semantics=("arbitrary",), vmem_limit_bytes=VMEM_LIMIT),
        name="merge",
    )(h, ya, ys, u, vg, gates, ln_g.reshape(1, -1), ln_b.reshape(1, -1), sgu_w, sgu_b.T,
      p_attn, p_ssm, p_sgu, w_out, post_g.reshape(1, d))


def kernel(x, ffn1_pre_g, ffn1_w_gu, ffn1_w_down, ffn1_post_g, mix_pre_g, w_in, conv_w, conv_b, dt_bias, a_log, d_skip, ssm_norm_g, sgu_ln_g, sgu_ln_b, sgu_w, sgu_b, p_attn, p_ssm, p_sgu, w_out, mix_post_g, ffn2_pre_g, ffn2_w_gu, ffn2_w_down, ffn2_post_g):
    b, s, d = x.shape
    depth = w_in.shape[0]
    h = x.reshape(b * s, d)
    as_b = lambda a: a.astype(BF16)
    for i in range(depth):
        h = _ffn(h, ffn1_pre_g[i], as_b(ffn1_w_gu[i]), as_b(ffn1_w_down[i]), ffn1_post_g[i])
        q, k, v, z, xbc, dt_raw, u, vg, gates = _inproj(h, mix_pre_g[i], _pad_w_in(w_in[i]))
        seq = lambda a: a.reshape(b, s, a.shape[-1])
        ya = _attention(seq(q), seq(k), seq(v))
        ys = _ssm(seq(xbc), seq(dt_raw), seq(z), conv_w[i], conv_b[i], dt_bias[i], a_log[i],
                  d_skip[i], ssm_norm_g[i])
        h = _merge(h, ya.reshape(b * s, -1), ys.reshape(b * s, -1), u, vg, gates,
                   sgu_ln_g[i], sgu_ln_b[i], sgu_w[i], sgu_b[i], as_b(p_attn[i]), as_b(p_ssm[i]),
                   as_b(p_sgu[i]), as_b(w_out[i]), mix_post_g[i])
        h = _ffn(h, ffn2_pre_g[i], as_b(ffn2_w_gu[i]), as_b(ffn2_w_down[i]), ffn2_post_g[i])
    return h.reshape(b, s, d)
```

```python
import functools

import jax
import jax.numpy as jnp
from jax import lax
from jax.experimental import pallas as pl
from jax.experimental.pallas import tpu as pltpu

F32 = jnp.float32
BF16 = jnp.bfloat16

NORM_EPS = 1e-6
ATTN_HEADS = 8
ATTN_HEAD_DIM = 128
MOBA_BLOCK = 256
MOBA_TOPK = 3
SSM_HEADS = 16
SSM_HEAD_DIM = 64
SSM_WIDTH = SSM_HEADS * SSM_HEAD_DIM
SSM_GROUPS = 2
SSM_STATE = 128
SSM_CONV = 4
SSM_CHUNK = 128
SSM_BC = SSM_GROUPS * SSM_STATE
SSM_CONV_CH = SSM_WIDTH + 2 * SSM_BC
SGU_GROUPS = 8
SGU_GROUP_DIM = 128
SGU_CHUNK = 128
N_BRANCHES = 3

LANES = 128
SUBLANES = 8
BF16_ROWS = 16
LOG2E = 1.4426950408889634
VMEM_LIMIT = 56 * 1024 * 1024
NEG_BIG = -1e30
POS_BIG = 1e30

FFN_TOKENS = 512
FFN_HIDDEN_CHUNK = 256
INPROJ_TOKENS = 256
INPROJ_COL_CHUNK = 256
SSM_TOKENS = 512
MERGE_TOKENS = 256
ATTN_GROUP = 8
ATTN_TILES_PER_STEP = 2


def _resident(shape):
    nd = len(shape)
    return pl.BlockSpec(shape, lambda *_: (0,) * nd, pipeline_mode=pl.Buffered(1))


def _rms(x, g):
    return x * lax.rsqrt(jnp.mean(x * x, axis=-1, keepdims=True) + NORM_EPS) * g


def _sigmoid(x):
    return 1.0 / (1.0 + jnp.exp(-x))


def _gelu_tanh(x):
    return x * (0.5 * (1.0 + jnp.tanh(0.7978845608028654 * (x + 0.044715 * (x * x * x)))))


def _dot(a, b):
    return jnp.dot(a, b, preferred_element_type=F32)


def _dot_nt(a, b):
    return lax.dot_general(a, b, (((1,), (1,)), ((), ())), preferred_element_type=F32)


def _dot_tn(a, b):
    return lax.dot_general(a, b, (((0,), (0,)), ((), ())), preferred_element_type=F32)


def _split2(x):
    hi = x.astype(BF16)
    lo = (x - hi.astype(F32)).astype(BF16)
    return hi, lo


def _split3(x):
    hi = x.astype(BF16)
    r = x - hi.astype(F32)
    mid = r.astype(BF16)
    lo = (r - mid.astype(F32)).astype(BF16)
    return hi, mid, lo


def _ffn_kernel(h_ref, pre_ref, wgu_ref, wd_ref, post_ref, o_ref, acc_ref):
    x = h_ref[...]
    xn = _rms(x, pre_ref[...]).astype(BF16)
    hidden = wd_ref.shape[0]
    fc = FFN_HIDDEN_CHUNK
    for c in range(hidden // fc):
        g = _dot(xn, wgu_ref[:, c * fc:(c + 1) * fc])
        u = _dot(xn, wgu_ref[:, hidden + c * fc:hidden + (c + 1) * fc])
        a = (g * _sigmoid(g) * u).astype(BF16)
        d = _dot(a, wd_ref[c * fc:(c + 1) * fc, :])
        if c == 0:
            acc_ref[...] = d
        else:
            acc_ref[...] += d
    o_ref[...] = x + 0.5 * _rms(acc_ref[...], post_ref[...])


def _ffn(h, pre_g, w_gu, w_down, post_g):
    t, d = h.shape
    hidden = w_down.shape[0]
    assert hidden % FFN_HIDDEN_CHUNK == 0 and t % FFN_TOKENS == 0
    tm = FFN_TOKENS
    return pl.pallas_call(
        _ffn_kernel,
        out_shape=jax.ShapeDtypeStruct((t, d), F32),
        grid=(t // tm,),
        in_specs=[
            pl.BlockSpec((tm, d), lambda i: (i, 0)),
            _resident((1, d)),
            _resident((d, 2 * hidden)),
            _resident((hidden, d)),
            _resident((1, d)),
        ],
        out_specs=pl.BlockSpec((tm, d), lambda i: (i, 0)),
        scratch_shapes=[pltpu.VMEM((tm, d), F32)],
        compiler_params=pltpu.CompilerParams(
            dimension_semantics=("arbitrary",), vmem_limit_bytes=VMEM_LIMIT),
        name="ffn",
    )(h, pre_g.reshape(1, d), w_gu, w_down, post_g.reshape(1, d))


def _inproj_layout(d_model):
    aw = ATTN_HEADS * ATTN_HEAD_DIM
    sw = SGU_GROUPS * SGU_GROUP_DIM
    return (("q", aw, aw), ("k", aw, aw), ("v", aw, aw), ("z", SSM_WIDTH, SSM_WIDTH),
            ("xbc", SSM_CONV_CH, SSM_CONV_CH), ("dt", SSM_HEADS, LANES),
            ("u", sw, sw), ("vg", sw, sw), ("gates", N_BRANCHES * d_model, N_BRANCHES * d_model))


def _inproj_kernel(h_ref, pre_ref, w_ref, convw_ref, convb_ref, lng_ref, lnb_ref,
                   q_ref, k_ref, v_ref, zs_ref, xc_ref, dt_ref, ua_ref, vn_ref, gs_ref,
                   xraw_ref, va_ref, xn_ref, *, tiles_per_seq):
    tm = h_ref.shape[0]
    pad = SUBLANES

    @pl.when(pl.program_id(0) % tiles_per_seq == 0)
    def _():
        xraw_ref[0:pad, :] = jnp.zeros((pad, SSM_CONV_CH), F32)

    xn_ref[...] = _rms(h_ref[...], pre_ref[...]).astype(BF16)

    def store_q(cols, r):
        q_ref[:, cols] = (r * (ATTN_HEAD_DIM ** -0.5 * LOG2E)).astype(BF16)

    def store_plain(o_ref):
        def store(cols, r):
            o_ref[:, cols] = r.astype(o_ref.dtype)
        return store

    def store_silu(cols, r):
        zs_ref[:, cols] = (r * _sigmoid(r)).astype(BF16)

    def store_conv(cols, r):
        xraw_ref[pad:pad + tm, cols] = r
        conv = convb_ref[:, cols] + convw_ref[SSM_CONV - 1:SSM_CONV, cols] * r
        for kk in range(SSM_CONV - 1):
            lag = SSM_CONV - 1 - kk
            conv = conv + convw_ref[kk:kk + 1, cols] * xraw_ref[pad - lag:pad - lag + tm, cols]
        xc_ref[:, cols] = (conv * _sigmoid(conv)).astype(BF16)

    def store_gelu(cols, r):
        ua_ref[:, cols] = _gelu_tanh(r).astype(BF16)

    def store_gelu_f32(cols, r):
        va_ref[:, cols] = _gelu_tanh(r)

    def store_sigmoid(cols, r):
        gs_ref[:, cols] = _sigmoid(r).astype(BF16)

    def layer_norm_v():
        va = va_ref[...]
        mu = jnp.mean(va, axis=-1, keepdims=True)
        var = jnp.mean(jnp.square(va - mu), axis=-1, keepdims=True)
        vn = (va - mu) * lax.rsqrt(var + NORM_EPS) * lng_ref[...] + lnb_ref[...]
        vn_ref[...] = vn.astype(BF16)

    def carry_rows():
        xraw_ref[0:pad, :] = xraw_ref[tm:tm + pad, :]

    stores = dict(q=store_q, k=store_plain(k_ref), v=store_plain(v_ref), z=store_silu,
                  xbc=store_conv, dt=store_plain(dt_ref), u=store_gelu, vg=store_gelu_f32,
                  gates=store_sigmoid)
    after = dict(vg=layer_norm_v, xbc=carry_rows)
    offsets, off = {}, 0
    for name, _, width in _inproj_layout(h_ref.shape[1]):
        offsets[name] = (off, width)
        off += width
    for name in ("vg", "xbc", "u", "gates", "z", "q", "k", "v", "dt"):
        off, width = offsets[name]
        for c0 in range(0, width, INPROJ_COL_CHUNK):
            n = min(INPROJ_COL_CHUNK, width - c0)
            stores[name](slice(c0, c0 + n), _dot(xn_ref[...], w_ref[:, off + c0:off + c0 + n]))
        if name in after:
            after[name]()


def _inproj(h, pre_g, w_pad, conv_w, conv_b, ln_g, ln_b, seq_len):
    t, d = h.shape
    tm = INPROJ_TOKENS
    layout = _inproj_layout(d)
    sw = SGU_GROUPS * SGU_GROUP_DIM
    assert seq_len % tm == 0 and t % seq_len == 0
    assert w_pad.shape[1] == sum(p for _, _, p in layout)
    out_shape = tuple(jax.ShapeDtypeStruct((t, p), F32 if name == "dt" else BF16)
                      for name, _, p in layout)
    out_specs = tuple(pl.BlockSpec((tm, p), lambda i: (i, 0)) for _, _, p in layout)
    return pl.pallas_call(
        functools.partial(_inproj_kernel, tiles_per_seq=seq_len // tm),
        out_shape=out_shape,
        grid=(t // tm,),
        in_specs=[
            pl.BlockSpec((tm, d), lambda i: (i, 0)),
            _resident((1, d)),
            _resident(w_pad.shape),
            _resident((SSM_CONV, SSM_CONV_CH)), _resident((1, SSM_CONV_CH)),
            _resident((1, sw)), _resident((1, sw)),
        ],
        out_specs=out_specs,
        scratch_shapes=[
            pltpu.VMEM((tm + SUBLANES, SSM_CONV_CH), F32),
            pltpu.VMEM((tm, sw), F32),
            pltpu.VMEM((tm, d), BF16),
        ],
        compiler_params=pltpu.CompilerParams(
            dimension_semantics=("arbitrary",), vmem_limit_bytes=VMEM_LIMIT),
        name="inproj",
    )(h, pre_g.reshape(1, d), w_pad, conv_w, conv_b.reshape(1, -1), ln_g.reshape(1, -1),
      ln_b.reshape(1, -1))


def _pad_w_in(w_in):
    d = w_in.shape[0]
    pieces, off = [], 0
    for _, width, padded in _inproj_layout(d):
        pieces.append(w_in[:, off:off + width])
        if padded > width:
            pieces.append(jnp.zeros((d, padded - width), w_in.dtype))
        off += width
    assert off == w_in.shape[1]
    return jnp.concatenate(pieces, axis=1).astype(BF16)


def _attn_kernel(slope_ref, q_ref, k_ref, v_ref, o_ref, vt_ref, kmean_ref, sel_ref, bias_ref,
                 s_ref, cmax_ref, p_ref, m_ref, acc_ref):
    blk = MOBA_BLOCK
    dh = ATTN_HEAD_DIM
    seq = q_ref.shape[1]
    nb = seq // blk
    gk = bias_ref.shape[0]
    group = gk // blk
    slope = slope_ref[0, 0:1, :] * LOG2E
    slope_row = jnp.concatenate([slope] * (blk // LANES), axis=1)
    bias_ref[...] = lax.broadcasted_iota(jnp.int32, (gk, blk), 0).astype(F32) * slope_row
    m_ref[...] = jnp.full(m_ref.shape, NEG_BIG, F32)
    acc_ref[...] = jnp.zeros_like(acc_ref)
    extra = lax.broadcasted_iota(jnp.int32, (vt_ref.shape[0] - dh, seq), 0)
    vt_ref[dh:, :] = jnp.where(extra == 0, 1.0, 0.0).astype(BF16)

    def prologue(jb, carry):
        off = pl.multiple_of(jb * blk, blk)
        kb = k_ref[0, pl.ds(off, blk), :].astype(F32)
        kmean_ref[pl.ds(jb, 1), :] = jnp.sum(kb, axis=0, keepdims=True) * (1.0 / blk)
        vb = v_ref[0, pl.ds(off, blk), :].astype(F32)
        vt_ref[0:dh, pl.ds(off, blk)] = vb.T.astype(BF16)
        return carry

    lax.fori_loop(0, nb, prologue, 0, unroll=2)

    km_hi, km_lo = _split2(kmean_ref[...])
    q_all = q_ref[0]
    gate = _dot_nt(km_hi, q_all) + _dot_nt(km_lo, q_all)
    blk_ids = lax.broadcasted_iota(jnp.int32, (nb, seq), 0)
    own_blk = lax.shift_right_logical(lax.broadcasted_iota(jnp.int32, (nb, seq), 1),
                                      blk.bit_length() - 1)
    gate = jnp.where(blk_ids < own_blk, gate, -jnp.inf)
    sel = jnp.where(blk_ids == own_blk, 1.0, 0.0)
    blk_ids = blk_ids.astype(F32)
    for _ in range(MOBA_TOPK):
        mx = jnp.max(gate, axis=0, keepdims=True)
        first = jnp.min(jnp.where(gate == mx, blk_ids, float(nb)), axis=0, keepdims=True)
        hit = blk_ids == first
        sel = jnp.where(jnp.logical_and(hit, mx > -jnp.inf), 1.0, sel)
        gate = jnp.where(hit, -jnp.inf, gate)
    sel_ref[...] = sel

    key_minus_qry = (lax.broadcasted_iota(jnp.int32, (blk, blk), 0)
                     - lax.broadcasted_iota(jnp.int32, (blk, blk), 1))

    def score_tile(g, step, slot, nblocks=group, own_block=None):
        r0 = pl.multiple_of(g * gk, gk)
        i = jnp.minimum(step, nb - 1)
        qi = q_ref[0, pl.ds(pl.multiple_of(i * blk, blk), blk), :]
        cms = []
        for u in range(nblocks):
            rows = slice(u * blk, (u + 1) * blk)
            s = _dot_nt(k_ref[0, pl.ds(r0 + u * blk, blk), :], qi) + bias_ref[rows, :]
            if u == own_block:
                s = jnp.where(key_minus_qry <= 0, s, NEG_BIG)
            s_ref[slot, rows, :] = s
            cms.append(jnp.max(s, axis=0, keepdims=True))
        cms += [jnp.full((1, blk), NEG_BIG, F32)] * (group - nblocks)
        cmax_ref[slot] = jnp.concatenate(cms, axis=0)

    def softmax_tile(g, i, slot, nblocks=group):
        r0 = pl.multiple_of(g * gk, gk)
        off = slope_row * jnp.asarray(g * gk - i * blk, F32)
        chosen = sel_ref[pl.ds(pl.multiple_of(g * group, group), group),
                         pl.ds(pl.multiple_of(i * blk, blk), blk)] > 0.0
        m_old = m_ref[pl.ds(i, 1), :]
        m_new = jnp.maximum(m_old, jnp.max(jnp.where(chosen, cmax_ref[slot] + off, NEG_BIG),
                                           axis=0, keepdims=True))
        base = m_new - off
        for u in range(nblocks):
            shift = jnp.where(chosen[u:u + 1, :], base, POS_BIG)
            pu = jnp.exp2(s_ref[slot, u * blk:(u + 1) * blk, :] - shift)
            p_ref[u * blk:(u + 1) * blk, :] = pu.astype(BF16)
        pv = _dot(vt_ref[:, pl.ds(r0, nblocks * blk)], p_ref[0:nblocks * blk, :])
        acc_ref[i] = jnp.exp2(m_old - m_new) * acc_ref[i] + pv
        m_ref[pl.ds(i, 1), :] = m_new

    def key_group(g, carry):
        first = g * group
        score_tile(g, first, 0, 1, 0)
        for r in range(group):
            if r + 1 < group:
                score_tile(g, first + r + 1, (r + 1) % 2, r + 2, r + 1)
            else:
                score_tile(g, first + group, group % 2)
            softmax_tile(g, first + r, r % 2, r + 1)

        def later_tiles(t, c):
            for r in range(ATTN_TILES_PER_STEP):
                i = first + ATTN_TILES_PER_STEP * t + r
                score_tile(g, i + 1, (r + 1) % 2)
                softmax_tile(g, i, r % 2)
            return c

        lax.fori_loop(group // ATTN_TILES_PER_STEP, (nb - first) // ATTN_TILES_PER_STEP,
                      later_tiles, 0)
        return carry

    lax.fori_loop(0, nb // group, key_group, 0)

    def finalize(i, carry):
        a = acc_ref[i]
        out = a[0:dh, :] / a[dh:dh + 1, :]
        o_ref[0, pl.ds(pl.multiple_of(i * blk, blk), blk), :] = out.T.astype(o_ref.dtype)
        return carry

    lax.fori_loop(0, nb, finalize, 0, unroll=2)


def _attention(q, k, v):
    b, s, w = q.shape
    dh = ATTN_HEAD_DIM
    nh = w // dh
    assert s % MOBA_BLOCK == 0 and s // MOBA_BLOCK >= MOBA_TOPK and dh == LANES
    nb = s // MOBA_BLOCK
    group = min(ATTN_GROUP, nb)
    assert nb % group == 0 and group % ATTN_TILES_PER_STEP == 0 and ATTN_TILES_PER_STEP % 2 == 0
    gk = group * MOBA_BLOCK
    slopes = 2.0 **(-8.0 * (jnp.arange(nh, dtype=F32) + 1.0) / nh)
    slopes = jnp.broadcast_to(slopes[:, None, None], (nh, SUBLANES, LANES))
    head_spec = pl.BlockSpec((1, s, dh), lambda bi, hi: (bi, 0, hi))
    return pl.pallas_call(
        _attn_kernel,
        out_shape=jax.ShapeDtypeStruct((b, s, w), BF16),
        grid=(b, nh),
        in_specs=[pl.BlockSpec((1, SUBLANES, LANES), lambda bi, hi: (hi, 0, 0)),
                  head_spec, head_spec, head_spec],
        out_specs=head_spec,
        scratch_shapes=[
            pltpu.VMEM((dh + BF16_ROWS, s), BF16),
            pltpu.VMEM((nb, dh), F32),
            pltpu.VMEM((nb, s), F32),
            pltpu.VMEM((gk, MOBA_BLOCK), F32),
            pltpu.VMEM((2, gk, MOBA_BLOCK), F32),
            pltpu.VMEM((2, group, MOBA_BLOCK), F32),
            pltpu.VMEM((gk, MOBA_BLOCK), BF16),
            pltpu.VMEM((nb, MOBA_BLOCK), F32),
            pltpu.VMEM((nb, dh + BF16_ROWS, MOBA_BLOCK), F32),
        ],
        compiler_params=pltpu.CompilerParams(
            dimension_semantics=("arbitrary", "arbitrary"), vmem_limit_bytes=VMEM_LIMIT),
        name="moba_attention",
    )(slopes, q, k, v)


def _ssm_kernel(xc_ref, dt_ref, zs_ref, dtb_ref, alog_ref, dskip_ref,
                normg_ref, expand_ref, o_ref, state_ref, y_ref):
    ts = xc_ref.shape[1]
    q = SSM_CHUNK
    hp = SSM_HEAD_DIM
    gw = SSM_WIDTH // SSM_GROUPS
    hpg = SSM_HEADS // SSM_GROUPS

    @pl.when(pl.program_id(1) == 0)
    def _():
        state_ref[...] = jnp.zeros_like(state_ref)

    a_neg = -jnp.exp(alog_ref[...])
    expand = expand_ref[...]
    row = lax.broadcasted_iota(jnp.int32, (q, q), 0)
    col = lax.broadcasted_iota(jnp.int32, (q, q), 1)
    lower = row >= col
    tri = jnp.where(lower, 1.0, 0.0).astype(BF16)

    def expand_heads(w):
        hi, lo = _split2(w)
        return _dot(hi, expand) + _dot(lo, expand)

    def chunk(c, carry):
        r0 = pl.multiple_of(c * q, q)
        rows = pl.ds(r0, q)
        xs_b = xc_ref[0, rows, 0:SSM_WIDTH]
        xs = xs_b.astype(F32)
        dtr = dt_ref[0, rows, :] + dtb_ref[...]
        dt = jnp.maximum(dtr, 0.0) + jnp.log1p(jnp.exp(-jnp.abs(dtr)))
        da = dt * a_neg
        d_hi, d_mid, d_lo = _split3(da)
        a_cs = _dot(tri, d_hi) + _dot(tri, d_mid) + _dot(tri, d_lo)
        a_cs_t = a_cs.T
        dt_t = dt.T
        a_last = a_cs[q - 1:q, :]

        ea_x = expand_heads(jnp.exp(a_cs))
        wend_x = expand_heads(jnp.exp(a_last - a_cs) * dt)
        cdec_x = ea_x[q - 1:q, :]
        xs_w = (xs * wend_x).astype(BF16)

        for g in range(SSM_GROUPS):
            bg = xc_ref[0, rows, SSM_WIDTH + g * SSM_STATE:SSM_WIDTH + (g + 1) * SSM_STATE]
            cg = xc_ref[0, rows, SSM_WIDTH + SSM_BC + g * SSM_STATE:
                        SSM_WIDTH + SSM_BC + (g + 1) * SSM_STATE]
            cb = _dot_nt(cg, bg)
            lanes = slice(g * gw, (g + 1) * gw)
            st = state_ref[:, lanes]
            y_ref[:, lanes] = _dot(cg, st.astype(BF16)) * ea_x[:, lanes]
            state_ref[:, lanes] = st * cdec_x[:, lanes] + _dot_tn(bg, xs_w[:, lanes])
            for hh in range(0, hpg, 2):
                pair = []
                for h in (g * hpg + hh, g * hpg + hh + 1):
                    seg = a_cs[:, h:h + 1] - a_cs_t[h:h + 1, :]
                    wts = cb * jnp.where(lower, jnp.exp(seg), 0.0) * dt_t[h:h + 1, :]
                    pair.append(_dot(wts.astype(BF16), xs_b[:, h * hp:(h + 1) * hp]))
                h0 = (g * hpg + hh) * hp
                y_ref[:, h0:h0 + 2 * hp] += jnp.concatenate(pair, axis=1)

        y = (y_ref[...] + dskip_ref[...] * xs) * zs_ref[0, rows, :].astype(F32)
        o_ref[0, rows, :] = _rms(y, normg_ref[...]).astype(o_ref.dtype)
        return carry

    lax.fori_loop(0, ts // q, chunk, 0)


def _ssm(xc, dt_raw, zs, dt_bias, a_log, d_skip, norm_g):
    b, s, _ = xc.shape
    ts = min(SSM_TOKENS, s)
    assert s % ts == 0 and ts % SSM_CHUNK == 0 and SSM_HEADS <= LANES

    def lane_pad(x):
        return jnp.zeros((1, LANES), F32).at[0, :SSM_HEADS].set(x)

    head_of_lane = jnp.arange(SSM_WIDTH) // SSM_HEAD_DIM
    expand = (jnp.arange(LANES)[:, None] == head_of_lane[None, :]).astype(BF16)
    tile = lambda w: pl.BlockSpec((1, ts, w), lambda bi, ti: (bi, ti, 0))
    return pl.pallas_call(
        _ssm_kernel,
        out_shape=jax.ShapeDtypeStruct((b, s, SSM_WIDTH), BF16),
        grid=(b, s // ts),
        in_specs=[
            tile(SSM_CONV_CH), tile(LANES), tile(SSM_WIDTH),
            _resident((1, LANES)), _resident((1, LANES)),
            _resident((1, SSM_WIDTH)), _resident((1, SSM_WIDTH)),
            _resident((LANES, SSM_WIDTH)),
        ],
        out_specs=tile(SSM_WIDTH),
        scratch_shapes=[
            pltpu.VMEM((SSM_STATE, SSM_WIDTH), F32),
            pltpu.VMEM((SSM_CHUNK, SSM_WIDTH), F32),
        ],
        compiler_params=pltpu.CompilerParams(
            dimension_semantics=("arbitrary", "arbitrary"), vmem_limit_bytes=VMEM_LIMIT),
        name="ssd",
    )(xc, dt_raw, zs, lane_pad(dt_bias), lane_pad(a_log),
      jnp.repeat(d_skip, SSM_HEAD_DIM).reshape(1, -1), norm_g.reshape(1, -1), expand)


def _merge_kernel(h_ref, ya_ref, ys_ref, ua_ref, vn_ref, gs_ref, sw_ref,
                  sbt_ref, pa_ref, ps_ref, pc_ref, wo_ref, post_ref, o_ref, yc_ref):
    tm, d = h_ref.shape
    ch, gd = SGU_CHUNK, SGU_GROUP_DIM

    row = lax.broadcasted_iota(jnp.int32, (ch, ch), 0)
    col = lax.broadcasted_iota(jnp.int32, (ch, ch), 1)
    for g in range(SGU_GROUPS):
        w_sp = jnp.where(row >= col, sw_ref[g], 0.0).astype(BF16)
        bias = sbt_ref[:, g:g + 1]
        for c in range(tm // ch):
            rs, cs = slice(c * ch, (c + 1) * ch), slice(g * gd, (g + 1) * gd)
            sv = _dot(w_sp, vn_ref[rs, cs]) + bias
            yc_ref[rs, cs] = (ua_ref[rs, cs].astype(F32) * sv).astype(BF16)

    merged = gs_ref[:, 0:d].astype(F32) * _dot(ya_ref[...], pa_ref[...])
    merged += gs_ref[:, d:2 * d].astype(F32) * _dot(ys_ref[...], ps_ref[...])
    merged += gs_ref[:, 2 * d:3 * d].astype(F32) * _dot(yc_ref[...], pc_ref[...])
    m = _dot(merged.astype(BF16), wo_ref[...])
    o_ref[...] = h_ref[...] + _rms(m, post_ref[...])


def _merge(h, ya, ys, ua, vn, gs, sgu_w, sgu_b, p_attn, p_ssm, p_sgu, w_out, post_g):
    t, d = h.shape
    tm = MERGE_TOKENS
    sw = SGU_GROUPS * SGU_GROUP_DIM
    assert t % tm == 0 and tm % SGU_CHUNK == 0
    tile = lambda w: pl.BlockSpec((tm, w), lambda i: (i, 0))
    return pl.pallas_call(
        _merge_kernel,
        out_shape=jax.ShapeDtypeStruct((t, d), F32),
        grid=(t // tm,),
        in_specs=[
            tile(d), tile(ya.shape[1]), tile(ys.shape[1]), tile(sw), tile(sw), tile(N_BRANCHES * d),
            _resident(sgu_w.shape), _resident((SGU_CHUNK, SGU_GROUPS)),
            _resident(p_attn.shape), _resident(p_ssm.shape), _resident(p_sgu.shape),
            _resident(w_out.shape), _resident((1, d)),
        ],
        out_specs=tile(d),
        scratch_shapes=[pltpu.VMEM((tm, sw), BF16)],
        compiler_params=pltpu.CompilerParams(
            dimension_semantics=("arbitrary",), vmem_limit_bytes=VMEM_LIMIT),
        name="merge",
    )(h, ya, ys, ua, vn, gs, sgu_w, sgu_b.T, p_attn, p_ssm, p_sgu, w_out, post_g.reshape(1, d))


def kernel(x, ffn1_pre_g, ffn1_w_gu, ffn1_w_down, ffn1_post_g, mix_pre_g, w_in, conv_w, conv_b, dt_bias, a_log, d_skip, ssm_norm_g, sgu_ln_g, sgu_ln_b, sgu_w, sgu_b, p_attn, p_ssm, p_sgu, w_out, mix_post_g, ffn2_pre_g, ffn2_w_gu, ffn2_w_down, ffn2_post_g):
    b, s, d = x.shape
    depth = w_in.shape[0]
    h = x.reshape(b * s, d)
    as_b = lambda a: a.astype(BF16)
    for i in range(depth):
        h = _ffn(h, ffn1_pre_g[i], as_b(ffn1_w_gu[i]), as_b(ffn1_w_down[i]), ffn1_post_g[i])
        q, k, v, zs, xc, dt_raw, ua, vn, gs = _inproj(
            h, mix_pre_g[i], _pad_w_in(w_in[i]), conv_w[i], conv_b[i], sgu_ln_g[i], sgu_ln_b[i], s)
        seq = lambda a: a.reshape(b, s, a.shape[-1])
        ya = _attention(seq(q), seq(k), seq(v))
        ys = _ssm(seq(xc), seq(dt_raw), seq(zs), dt_bias[i], a_log[i], d_skip[i], ssm_norm_g[i])
        h = _merge(h, ya.reshape(b * s, -1), ys.reshape(b * s, -1), ua, vn, gs,
                   sgu_w[i], sgu_b[i], as_b(p_attn[i]), as_b(p_ssm[i]),
                   as_b(p_sgu[i]), as_b(w_out[i]), mix_post_g[i])
        h = _ffn(h, ffn2_pre_g[i], as_b(ffn2_w_gu[i]), as_b(ffn2_w_down[i]), ffn2_post_g[i])
    return h.reshape(b, s, d)
```

```python
import functools

import jax
import jax.numpy as jnp
from jax import lax
from jax.experimental import pallas as pl
from jax.experimental.pallas import tpu as pltpu

F32 = jnp.float32
BF16 = jnp.bfloat16

NORM_EPS = 1e-6
ATTN_HEADS = 8
ATTN_HEAD_DIM = 128
MOBA_BLOCK = 256
MOBA_TOPK = 3
SSM_HEADS = 16
SSM_HEAD_DIM = 64
SSM_WIDTH = SSM_HEADS * SSM_HEAD_DIM
SSM_GROUPS = 2
SSM_STATE = 128
SSM_CONV = 4
SSM_CHUNK = 128
SSM_BC = SSM_GROUPS * SSM_STATE
SSM_CONV_CH = SSM_WIDTH + 2 * SSM_BC
SGU_GROUPS = 8
SGU_GROUP_DIM = 128
SGU_CHUNK = 128
N_BRANCHES = 3

LANES = 128
SUBLANES = 8
BF16_ROWS = 16
LOG2E = 1.4426950408889634
VMEM_LIMIT = 56 * 1024 * 1024
NEG_BIG = -1e30
POS_BIG = 1e30

FFN_TOKENS = 512
FFN_HIDDEN_CHUNK = 256
INPROJ_TOKENS = 256
INPROJ_COL_CHUNK = 256
SSM_TOKENS = 512
MERGE_TOKENS = 256
ATTN_GROUP = 8

def _resident(shape):
    nd = len(shape)
    return pl.BlockSpec(shape, lambda *_: (0,) * nd, pipeline_mode=pl.Buffered(1))


def _resident_layer(stacked_shape, layer):
    shape = tuple(stacked_shape[1:])
    nd = len(shape)
    return pl.BlockSpec((None,) + shape, lambda *_: (layer,) + (0,) * nd,
                        pipeline_mode=pl.Buffered(1))


def _rms(x, g):
    return x * lax.rsqrt(jnp.mean(x * x, axis=-1, keepdims=True) + NORM_EPS) * g


def _sigmoid(x):
    return 1.0 / (1.0 + jnp.exp(-x))


def _gelu_tanh(x):
    return x * (0.5 * (1.0 + jnp.tanh(0.7978845608028654 * (x + 0.044715 * (x * x * x)))))


def _dot(a, b):
    return jnp.dot(a, b, preferred_element_type=F32)


def _dot_nt(a, b):
    return lax.dot_general(a, b, (((1,), (1,)), ((), ())), preferred_element_type=F32)


def _dot_tn(a, b):
    return lax.dot_general(a, b, (((0,), (0,)), ((), ())), preferred_element_type=F32)


def _split2(x):
    hi = x.astype(BF16)
    lo = (x - hi.astype(F32)).astype(BF16)
    return hi, lo


def _split3(x):
    hi = x.astype(BF16)
    r = x - hi.astype(F32)
    mid = r.astype(BF16)
    lo = (r - mid.astype(F32)).astype(BF16)
    return hi, mid, lo


def _ffn_kernel(h_ref, pre_ref, wgu_ref, wd_ref, post_ref, o_ref, acc_ref):
    x = h_ref[...]
    xn = _rms(x, pre_ref[...]).astype(BF16)
    hidden = wd_ref.shape[0]
    fc = FFN_HIDDEN_CHUNK
    for c in range(hidden // fc):
        g = _dot(xn, wgu_ref[:, c * fc:(c + 1) * fc])
        u = _dot(xn, wgu_ref[:, hidden + c * fc:hidden + (c + 1) * fc])
        a = (g * _sigmoid(g) * u).astype(BF16)
        d = _dot(a, wd_ref[c * fc:(c + 1) * fc, :])
        if c == 0:
            acc_ref[...] = d
        else:
            acc_ref[...] += d
    o_ref[...] = x + 0.5 * _rms(acc_ref[...], post_ref[...])


def _ffn(h, pre_g, w_gu, w_down, post_g, layer):
    t, d = h.shape
    hidden = w_down.shape[1]
    assert hidden % FFN_HIDDEN_CHUNK == 0 and t % FFN_TOKENS == 0
    tm = FFN_TOKENS
    return pl.pallas_call(
        _ffn_kernel,
        out_shape=jax.ShapeDtypeStruct((t, d), F32),
        grid=(t // tm,),
        in_specs=[
            pl.BlockSpec((tm, d), lambda i: (i, 0)),
            _resident((1, d)),
            _resident_layer(w_gu.shape, layer),
            _resident_layer(w_down.shape, layer),
            _resident((1, d)),
        ],
        out_specs=pl.BlockSpec((tm, d), lambda i: (i, 0)),
        scratch_shapes=[pltpu.VMEM((tm, d), F32)],
        compiler_params=pltpu.CompilerParams(
            dimension_semantics=("arbitrary",), vmem_limit_bytes=VMEM_LIMIT),
        name="ffn",
    )(h, pre_g.reshape(1, d), w_gu, w_down, post_g.reshape(1, d))


def _inproj_layout(d_model):
    aw = ATTN_HEADS * ATTN_HEAD_DIM
    sw = SGU_GROUPS * SGU_GROUP_DIM
    return (("q", aw, aw), ("k", aw, aw), ("v", aw, aw), ("z", SSM_WIDTH, SSM_WIDTH),
            ("xbc", SSM_CONV_CH, SSM_CONV_CH), ("dt", SSM_HEADS, LANES),
            ("u", sw, sw), ("vg", sw, sw), ("gates", N_BRANCHES * d_model, N_BRANCHES * d_model))


def _inproj_kernel(h_ref, pre_ref, w_ref, convw_ref, convb_ref, lng_ref, lnb_ref,
                   q_ref, k_ref, v_ref, zs_ref, xc_ref, dt_ref, ua_ref, vn_ref, gs_ref,
                   xraw_ref, va_ref, xn_ref, *, tiles_per_seq):
    tm = h_ref.shape[0]
    pad = SUBLANES

    @pl.when(pl.program_id(0) % tiles_per_seq == 0)
    def _():
        xraw_ref[0:pad, :] = jnp.zeros((pad, SSM_CONV_CH), F32)

    xn_ref[...] = _rms(h_ref[...], pre_ref[...]).astype(BF16)

    def store_q(cols, r):
        q_ref[:, cols] = (r * (ATTN_HEAD_DIM ** -0.5 * LOG2E)).astype(BF16)

    def store_plain(o_ref):
        def store(cols, r):
            o_ref[:, cols] = r.astype(o_ref.dtype)
        return store

    def store_silu(cols, r):
        zs_ref[:, cols] = (r * _sigmoid(r)).astype(BF16)

    def store_conv(cols, r):
        xraw_ref[pad:pad + tm, cols] = r
        conv = convb_ref[:, cols] + convw_ref[SSM_CONV - 1:SSM_CONV, cols] * r
        for kk in range(SSM_CONV - 1):
            lag = SSM_CONV - 1 - kk
            conv = conv + convw_ref[kk:kk + 1, cols] * xraw_ref[pad - lag:pad - lag + tm, cols]
        xc_ref[:, cols] = (conv * _sigmoid(conv)).astype(BF16)

    def store_gelu(cols, r):
        ua_ref[:, cols] = _gelu_tanh(r).astype(BF16)

    def store_gelu_f32(cols, r):
        va_ref[:, cols] = _gelu_tanh(r)

    def store_sigmoid(cols, r):
        gs_ref[:, cols] = _sigmoid(r).astype(BF16)

    def layer_norm_v():
        va = va_ref[...]
        mu = jnp.mean(va, axis=-1, keepdims=True)
        var = jnp.mean(jnp.square(va - mu), axis=-1, keepdims=True)
        vn = (va - mu) * lax.rsqrt(var + NORM_EPS) * lng_ref[...] + lnb_ref[...]
        vn_ref[...] = vn.astype(BF16)

    def carry_rows():
        xraw_ref[0:pad, :] = xraw_ref[tm:tm + pad, :]

    stores = dict(q=store_q, k=store_plain(k_ref), v=store_plain(v_ref), z=store_silu,
                  xbc=store_conv, dt=store_plain(dt_ref), u=store_gelu, vg=store_gelu_f32,
                  gates=store_sigmoid)
    after = dict(vg=layer_norm_v, xbc=carry_rows)
    offsets, off = {}, 0
    for name, _, width in _inproj_layout(h_ref.shape[1]):
        offsets[name] = (off, width)
        off += width
    def chunks(names):
        out = []
        for name in names:
            off, width = offsets[name]
            starts = list(range(0, width, INPROJ_COL_CHUNK))
            for c0 in starts:
                out.append((name, off, c0, min(INPROJ_COL_CHUNK, width - c0), c0 == starts[-1]))
        return out

    heavy = chunks(("vg", "xbc", "u", "z"))
    light = chunks(("q", "k", "v", "dt", "gates"))
    per_heavy = len(light) // len(heavy)
    order = []
    while heavy or light:
        order += heavy[:1] + light[:per_heavy]
        heavy, light = heavy[1:], light[per_heavy:]
    for name, off, c0, n, last in order:
        stores[name](slice(c0, c0 + n), _dot(xn_ref[...], w_ref[:, off + c0:off + c0 + n]))
        if last and name in after:
            after[name]()


def _inproj(h, pre_g, w_pad, conv_w, conv_b, ln_g, ln_b, seq_len, layer):
    t, d = h.shape
    tm = INPROJ_TOKENS
    layout = _inproj_layout(d)
    sw = SGU_GROUPS * SGU_GROUP_DIM
    assert seq_len % tm == 0 and t % seq_len == 0
    assert w_pad.shape[2] == sum(p for _, _, p in layout)
    out_shape = tuple(jax.ShapeDtypeStruct((t, p), F32 if name == "dt" else BF16)
                      for name, _, p in layout)
    out_specs = tuple(pl.BlockSpec((tm, p), lambda i: (i, 0)) for _, _, p in layout)
    return pl.pallas_call(
        functools.partial(_inproj_kernel, tiles_per_seq=seq_len // tm),
        out_shape=out_shape,
        grid=(t // tm,),
        in_specs=[
            pl.BlockSpec((tm, d), lambda i: (i, 0)),
            _resident((1, d)),
            _resident_layer(w_pad.shape, layer),
            _resident((SSM_CONV, SSM_CONV_CH)), _resident((1, SSM_CONV_CH)),
            _resident((1, sw)), _resident((1, sw)),
        ],
        out_specs=out_specs,
        scratch_shapes=[
            pltpu.VMEM((tm + SUBLANES, SSM_CONV_CH), F32),
            pltpu.VMEM((tm, sw), F32),
            pltpu.VMEM((tm, d), BF16),
        ],
        compiler_params=pltpu.CompilerParams(
            dimension_semantics=("arbitrary",), vmem_limit_bytes=VMEM_LIMIT),
        name="inproj",
    )(h, pre_g.reshape(1, d), w_pad, conv_w, conv_b.reshape(1, -1), ln_g.reshape(1, -1),
      ln_b.reshape(1, -1))


def _pad_w_in(w_in):
    depth, d, cols = w_in.shape
    pieces, off = [], 0
    for _, width, padded in _inproj_layout(d):
        pieces.append(w_in[:, :, off:off + width].astype(BF16))
        if padded > width:
            pieces.append(jnp.zeros((depth, d, padded - width), BF16))
        off += width
    assert off == cols
    return jnp.concatenate(pieces, axis=2)


def _attn_kernel(slope_ref, q_ref, k_ref, v_ref, o_ref, vt_ref, kmean_ref, sel_ref, bias_ref,
                 s_ref, cmax_ref, p_ref, m_ref, acc_ref):
    blk = MOBA_BLOCK
    dh = ATTN_HEAD_DIM
    seq = q_ref.shape[1]
    nb = seq // blk
    gk = bias_ref.shape[0]
    group = gk // blk
    slope = slope_ref[0, 0:1, :] * LOG2E
    slope_row = jnp.concatenate([slope] * (blk // LANES), axis=1)
    bias_ref[...] = lax.broadcasted_iota(jnp.int32, (gk, blk), 0).astype(F32) * slope_row
    m_ref[...] = jnp.full(m_ref.shape, NEG_BIG, F32)
    acc_ref[...] = jnp.zeros_like(acc_ref)
    extra = lax.broadcasted_iota(jnp.int32, (vt_ref.shape[0] - dh, seq), 0)
    vt_ref[dh:, :] = jnp.where(extra == 0, 1.0, 0.0).astype(BF16)

    def prologue(jb, carry):
        off = pl.multiple_of(jb * blk, blk)
        kb = k_ref[0, pl.ds(off, blk), :].astype(F32)
        kmean_ref[pl.ds(jb, 1), :] = jnp.sum(kb, axis=0, keepdims=True) * (1.0 / blk)
        vb = v_ref[0, pl.ds(off, blk), :].astype(F32)
        vt_ref[0:dh, pl.ds(off, blk)] = vb.T.astype(BF16)
        return carry

    lax.fori_loop(0, nb, prologue, 0, unroll=2)

    km_hi, km_lo = _split2(kmean_ref[...])
    q_all = q_ref[0]
    gate = _dot_nt(km_hi, q_all) + _dot_nt(km_lo, q_all)
    blk_ids = lax.broadcasted_iota(jnp.int32, (nb, seq), 0)
    own_blk = lax.shift_right_logical(lax.broadcasted_iota(jnp.int32, (nb, seq), 1),
                                      blk.bit_length() - 1)
    gate = jnp.where(blk_ids < own_blk, gate, -jnp.inf)
    sel = jnp.where(blk_ids == own_blk, 1.0, 0.0)
    blk_ids = blk_ids.astype(F32)
    for _ in range(MOBA_TOPK):
        mx = jnp.max(gate, axis=0, keepdims=True)
        first = jnp.min(jnp.where(gate == mx, blk_ids, float(nb)), axis=0, keepdims=True)
        hit = blk_ids == first
        sel = jnp.where(jnp.logical_and(hit, mx > -jnp.inf), 1.0, sel)
        gate = jnp.where(hit, -jnp.inf, gate)
    sel_ref[...] = sel

    key_minus_qry = (lax.broadcasted_iota(jnp.int32, (blk, blk), 0)
                     - lax.broadcasted_iota(jnp.int32, (blk, blk), 1))

    def score_tile(g, step, slot, nblocks=group, own_block=None):
        r0 = pl.multiple_of(g * gk, gk)
        i = jnp.minimum(step, nb - 1)
        qi = q_ref[0, pl.ds(pl.multiple_of(i * blk, blk), blk), :]
        cms = []
        for u in range(nblocks):
            rows = slice(u * blk, (u + 1) * blk)
            s = _dot_nt(k_ref[0, pl.ds(r0 + u * blk, blk), :], qi) + bias_ref[rows, :]
            if u == own_block:
                s = jnp.where(key_minus_qry <= 0, s, NEG_BIG)
            s_ref[slot, rows, :] = s
            cms.append(jnp.max(s, axis=0, keepdims=True))
        cms += [jnp.full((1, blk), NEG_BIG, F32)] * (group - nblocks)
        cmax_ref[slot] = jnp.concatenate(cms, axis=0)

    def softmax_tile(g, i, slot, nblocks=group):
        r0 = pl.multiple_of(g * gk, gk)
        off = slope_row * jnp.asarray(g * gk - i * blk, F32)
        chosen = sel_ref[pl.ds(pl.multiple_of(g * group, group), group),
                         pl.ds(pl.multiple_of(i * blk, blk), blk)] > 0.0
        m_old = m_ref[pl.ds(i, 1), :]
        m_new = jnp.maximum(m_old, jnp.max(jnp.where(chosen, cmax_ref[slot] + off, NEG_BIG),
                                           axis=0, keepdims=True))
        base = m_new - off
        for u in range(nblocks):
            shift = jnp.where(chosen[u:u + 1, :], base, POS_BIG)
            x = (s_ref[slot, u * blk:(u + 1) * blk, :] - shift).astype(BF16)
            p_ref[u * blk:(u + 1) * blk, :] = jnp.exp2(x)
        pv = _dot(vt_ref[:, pl.ds(r0, nblocks * blk)], p_ref[0:nblocks * blk, :])
        acc_ref[i] = jnp.exp2(m_old - m_new) * acc_ref[i] + pv
        m_ref[pl.ds(i, 1), :] = m_new

    def key_group(g, carry):
        first = g * group
        score_tile(g, first, 0, 1, 0)
        for r in range(group):
            if r + 1 < group:
                score_tile(g, first + r + 1, (r + 1) % 2, r + 2, r + 1)
            else:
                score_tile(g, first + group, group % 2)
            softmax_tile(g, first + r, r % 2, r + 1)

        @pl.when(first + group < nb)
        def _():
            score_tile(g, first + group + 1, 1)

            def tile_pair(i, score_ahead=True):
                for slot in range(2):
                    softmax_tile(g, i + slot, slot)
                    if score_ahead:
                        score_tile(g, i + 2 + slot, slot)

            def later_tiles(t, c):
                tile_pair(first + 2 * t)
                return c

            lax.fori_loop(group // 2, (nb - first) // 2 - 1, later_tiles, 0)
            tile_pair(nb - 2, score_ahead=False)

        return carry

    lax.fori_loop(0, nb // group, key_group, 0)

    def finalize(i, carry):
        a = acc_ref[i]
        out = a[0:dh, :] / a[dh:dh + 1, :]
        o_ref[0, pl.ds(pl.multiple_of(i * blk, blk), blk), :] = out.T.astype(o_ref.dtype)
        return carry

    lax.fori_loop(0, nb, finalize, 0, unroll=2)


def _attention(q, k, v):
    b, s, w = q.shape
    dh = ATTN_HEAD_DIM
    nh = w // dh
    assert s % MOBA_BLOCK == 0 and s // MOBA_BLOCK >= MOBA_TOPK and dh == LANES
    nb = s // MOBA_BLOCK
    group = min(ATTN_GROUP, nb)
    assert nb % group == 0 and group % 2 == 0
    gk = group * MOBA_BLOCK
    slopes = 2.0 **(-8.0 * (jnp.arange(nh, dtype=F32) + 1.0) / nh)
    slopes = jnp.broadcast_to(slopes[:, None, None], (nh, SUBLANES, LANES))
    head_spec = pl.BlockSpec((1, s, dh), lambda bi, hi: (bi, 0, hi))
    return pl.pallas_call(
        _attn_kernel,
        out_shape=jax.ShapeDtypeStruct((b, s, w), BF16),
        grid=(b, nh),
        in_specs=[pl.BlockSpec((1, SUBLANES, LANES), lambda bi, hi: (hi, 0, 0)),
                  head_spec, head_spec, head_spec],
        out_specs=head_spec,
        scratch_shapes=[
            pltpu.VMEM((dh + BF16_ROWS, s), BF16),
            pltpu.VMEM((nb, dh), F32),
            pltpu.VMEM((nb, s), F32),
            pltpu.VMEM((gk, MOBA_BLOCK), F32),
            pltpu.VMEM((2, gk, MOBA_BLOCK), F32),
            pltpu.VMEM((2, group, MOBA_BLOCK), F32),
            pltpu.VMEM((gk, MOBA_BLOCK), BF16),
            pltpu.VMEM((nb, MOBA_BLOCK), F32),
            pltpu.VMEM((nb, dh + BF16_ROWS, MOBA_BLOCK), F32),
        ],
        compiler_params=pltpu.CompilerParams(
            dimension_semantics=("arbitrary", "arbitrary"), vmem_limit_bytes=VMEM_LIMIT),
        name="moba_attention",
    )(slopes, q, k, v)


def _ssm_kernel(xc_ref, dt_ref, zs_ref, dtb_ref, alog_ref, dskip_ref,
                normg_ref, expand_ref, o_ref, state_ref, y_ref):
    ts = xc_ref.shape[1]
    q = SSM_CHUNK
    hp = SSM_HEAD_DIM
    gw = SSM_WIDTH // SSM_GROUPS
    hpg = SSM_HEADS // SSM_GROUPS

    @pl.when(pl.program_id(1) == 0)
    def _():
        state_ref[...] = jnp.zeros_like(state_ref)

    a_neg = -jnp.exp(alog_ref[...])
    expand = expand_ref[...]
    row = lax.broadcasted_iota(jnp.int32, (q, q), 0)
    col = lax.broadcasted_iota(jnp.int32, (q, q), 1)
    lower = row >= col
    tri = jnp.where(lower, 1.0, 0.0).astype(BF16)

    def expand_heads(w):
        hi, lo = _split2(w)
        return _dot(hi, expand) + _dot(lo, expand)

    def chunk(c, carry):
        r0 = pl.multiple_of(c * q, q)
        rows = pl.ds(r0, q)
        xs_b = xc_ref[0, rows, 0:SSM_WIDTH]
        xs = xs_b.astype(F32)
        dtr = dt_ref[0, rows, :] + dtb_ref[...]
        dt = jnp.maximum(dtr, 0.0) + jnp.log1p(jnp.exp(-jnp.abs(dtr)))
        da = dt * a_neg
        d_hi, d_mid, d_lo = _split3(da)
        a_cs = _dot(tri, d_hi) + _dot(tri, d_mid) + _dot(tri, d_lo)
        a_cs_t = a_cs.T
        dt_t = dt.T
        a_last = a_cs[q - 1:q, :]

        ea_x = expand_heads(jnp.exp(a_cs))
        wend_x = expand_heads(jnp.exp(a_last - a_cs) * dt)
        cdec_x = ea_x[q - 1:q, :]
        xs_w = (xs * wend_x).astype(BF16)

        for g in range(SSM_GROUPS):
            bg = xc_ref[0, rows, SSM_WIDTH + g * SSM_STATE:SSM_WIDTH + (g + 1) * SSM_STATE]
            cg = xc_ref[0, rows, SSM_WIDTH + SSM_BC + g * SSM_STATE:
                        SSM_WIDTH + SSM_BC + (g + 1) * SSM_STATE]
            cb = _dot_nt(cg, bg)
            lanes = slice(g * gw, (g + 1) * gw)
            st = state_ref[:, lanes]
            y_ref[:, lanes] = _dot(cg, st.astype(BF16)) * ea_x[:, lanes]
            state_ref[:, lanes] = st * cdec_x[:, lanes] + _dot_tn(bg, xs_w[:, lanes])
            for hh in range(0, hpg, 2):
                pair = []
                for h in (g * hpg + hh, g * hpg + hh + 1):
                    seg = a_cs[:, h:h + 1] - a_cs_t[h:h + 1, :]
                    wts = cb * jnp.where(lower, jnp.exp(seg), 0.0) * dt_t[h:h + 1, :]
                    pair.append(_dot(wts.astype(BF16), xs_b[:, h * hp:(h + 1) * hp]))
                h0 = (g * hpg + hh) * hp
                y_ref[:, h0:h0 + 2 * hp] += jnp.concatenate(pair, axis=1)

        y = (y_ref[...] + dskip_ref[...] * xs) * zs_ref[0, rows, :].astype(F32)
        o_ref[0, rows, :] = _rms(y, normg_ref[...]).astype(o_ref.dtype)
        return carry

    lax.fori_loop(0, ts // q, chunk, 0, unroll=2)


def _ssm(xc, dt_raw, zs, dt_bias, a_log, d_skip, norm_g):
    b, s, _ = xc.shape
    ts = min(SSM_TOKENS, s)
    assert s % ts == 0 and ts % SSM_CHUNK == 0 and SSM_HEADS <= LANES

    def lane_pad(x):
        return jnp.zeros((1, LANES), F32).at[0, :SSM_HEADS].set(x)

    head_of_lane = jnp.arange(SSM_WIDTH) // SSM_HEAD_DIM
    expand = (jnp.arange(LANES)[:, None] == head_of_lane[None, :]).astype(BF16)
    tile = lambda w: pl.BlockSpec((1, ts, w), lambda bi, ti: (bi, ti, 0))
    return pl.pallas_call(
        _ssm_kernel,
        out_shape=jax.ShapeDtypeStruct((b, s, SSM_WIDTH), BF16),
        grid=(b, s // ts),
        in_specs=[
            tile(SSM_CONV_CH), tile(LANES), tile(SSM_WIDTH),
            _resident((1, LANES)), _resident((1, LANES)),
            _resident((1, SSM_WIDTH)), _resident((1, SSM_WIDTH)),
            _resident((LANES, SSM_WIDTH)),
        ],
        out_specs=tile(SSM_WIDTH),
        scratch_shapes=[
            pltpu.VMEM((SSM_STATE, SSM_WIDTH), F32),
            pltpu.VMEM((SSM_CHUNK, SSM_WIDTH), F32),
        ],
        compiler_params=pltpu.CompilerParams(
            dimension_semantics=("arbitrary", "arbitrary"), vmem_limit_bytes=VMEM_LIMIT),
        name="ssd",
    )(xc, dt_raw, zs, lane_pad(dt_bias), lane_pad(a_log),
      jnp.repeat(d_skip, SSM_HEAD_DIM).reshape(1, -1), norm_g.reshape(1, -1), expand)


def _merge_kernel(h_ref, ya_ref, ys_ref, ua_ref, vn_ref, gs_ref, sw_ref,
                  sbt_ref, pa_ref, ps_ref, pc_ref, wo_ref, post_ref, o_ref, yc_ref):
    tm, d = h_ref.shape
    ch, gd = SGU_CHUNK, SGU_GROUP_DIM

    row = lax.broadcasted_iota(jnp.int32, (ch, ch), 0)
    col = lax.broadcasted_iota(jnp.int32, (ch, ch), 1)
    for g in range(SGU_GROUPS):
        w_sp = jnp.where(row >= col, sw_ref[g], 0.0).astype(BF16)
        bias = sbt_ref[:, g:g + 1]
        for c in range(tm // ch):
            rs, cs = slice(c * ch, (c + 1) * ch), slice(g * gd, (g + 1) * gd)
            sv = _dot(w_sp, vn_ref[rs, cs]) + bias
            yc_ref[rs, cs] = (ua_ref[rs, cs].astype(F32) * sv).astype(BF16)

    merged = gs_ref[:, 0:d].astype(F32) * _dot(ya_ref[...], pa_ref[...])
    merged += gs_ref[:, d:2 * d].astype(F32) * _dot(ys_ref[...], ps_ref[...])
    merged += gs_ref[:, 2 * d:3 * d].astype(F32) * _dot(yc_ref[...], pc_ref[...])
    m = _dot(merged.astype(BF16), wo_ref[...])
    o_ref[...] = h_ref[...] + _rms(m, post_ref[...])


def _merge(h, ya, ys, ua, vn, gs, sgu_w, sgu_b, p_attn, p_ssm, p_sgu, w_out, post_g, layer):
    t, d = h.shape
    tm = MERGE_TOKENS
    sw = SGU_GROUPS * SGU_GROUP_DIM
    assert t % tm == 0 and tm % SGU_CHUNK == 0
    tile = lambda w: pl.BlockSpec((tm, w), lambda i: (i, 0))
    return pl.pallas_call(
        _merge_kernel,
        out_shape=jax.ShapeDtypeStruct((t, d), F32),
        grid=(t // tm,),
        in_specs=[
            tile(d), tile(ya.shape[1]), tile(ys.shape[1]), tile(sw), tile(sw), tile(N_BRANCHES * d),
            _resident(sgu_w.shape), _resident((SGU_CHUNK, SGU_GROUPS)),
            _resident_layer(p_attn.shape, layer), _resident_layer(p_ssm.shape, layer),
            _resident_layer(p_sgu.shape, layer), _resident_layer(w_out.shape, layer),
            _resident((1, d)),
        ],
        out_specs=tile(d),
        scratch_shapes=[pltpu.VMEM((tm, sw), BF16)],
        compiler_params=pltpu.CompilerParams(
            dimension_semantics=("arbitrary",), vmem_limit_bytes=VMEM_LIMIT),
        name="merge",
    )(h, ya, ys, ua, vn, gs, sgu_w, sgu_b.T, p_attn, p_ssm, p_sgu, w_out, post_g.reshape(1, d))


def kernel(x, ffn1_pre_g, ffn1_w_gu, ffn1_w_down, ffn1_post_g, mix_pre_g, w_in, conv_w, conv_b, dt_bias, a_log, d_skip, ssm_norm_g, sgu_ln_g, sgu_ln_b, sgu_w, sgu_b, p_attn, p_ssm, p_sgu, w_out, mix_post_g, ffn2_pre_g, ffn2_w_gu, ffn2_w_down, ffn2_post_g):
    b, s, d = x.shape
    depth = w_in.shape[0]
    h = x.reshape(b * s, d)
    as_b = lambda a: a.astype(BF16)
    ffn1_w_gu, ffn1_w_down, ffn2_w_gu, ffn2_w_down = map(
        as_b, (ffn1_w_gu, ffn1_w_down, ffn2_w_gu, ffn2_w_down))
    p_attn, p_ssm, p_sgu, w_out = map(as_b, (p_attn, p_ssm, p_sgu, w_out))
    w_pad = _pad_w_in(w_in)
    for i in range(depth):
        h = _ffn(h, ffn1_pre_g[i], ffn1_w_gu, ffn1_w_down, ffn1_post_g[i], i)
        q, k, v, zs, xc, dt_raw, ua, vn, gs = _inproj(
            h, mix_pre_g[i], w_pad, conv_w[i], conv_b[i], sgu_ln_g[i], sgu_ln_b[i], s, i)
        seq = lambda a: a.reshape(b, s, a.shape[-1])
        ya = _attention(seq(q), seq(k), seq(v))
        ys = _ssm(seq(xc), seq(dt_raw), seq(zs), dt_bias[i], a_log[i], d_skip[i], ssm_norm_g[i])
        h = _merge(h, ya.reshape(b * s, -1), ys.reshape(b * s, -1), ua, vn, gs,
                   sgu_w[i], sgu_b[i], p_attn, p_ssm, p_sgu, w_out, mix_post_g[i], i)
        h = _ffn(h, ffn2_pre_g[i], ffn2_w_gu, ffn2_w_down, ffn2_post_g[i], i)
    return h.reshape(b, s, d)
```

```python
import functools

import jax
import jax.numpy as jnp
from jax import lax
from jax.experimental import pallas as pl
from jax.experimental.pallas import tpu as pltpu

F32 = jnp.float32
BF16 = jnp.bfloat16

NORM_EPS = 1e-6
ATTN_HEADS = 8
ATTN_HEAD_DIM = 128
MOBA_BLOCK = 256
MOBA_TOPK = 3
SSM_HEADS = 16
SSM_HEAD_DIM = 64
SSM_WIDTH = SSM_HEADS * SSM_HEAD_DIM
SSM_GROUPS = 2
SSM_STATE = 128
SSM_CONV = 4
SSM_CHUNK = 128
SSM_BC = SSM_GROUPS * SSM_STATE
SSM_CONV_CH = SSM_WIDTH + 2 * SSM_BC
SGU_GROUPS = 8
SGU_GROUP_DIM = 128
SGU_CHUNK = 128
N_BRANCHES = 3

LANES = 128
SUBLANES = 8
BF16_ROWS = 16
LOG2E = 1.4426950408889634
VMEM_LIMIT = 56 * 1024 * 1024
NEG_BIG = -1e30
POS_BIG = 1e30

FFN_TOKENS = 512
FFN_HIDDEN_CHUNK = 256
INPROJ_TOKENS = 256
INPROJ_COL_CHUNK = 256
SSM_TOKENS = 512
MERGE_TOKENS = 512
ATTN_GROUP = 8

def _resident(shape):
    nd = len(shape)
    return pl.BlockSpec(shape, lambda *_: (0,) * nd, pipeline_mode=pl.Buffered(1))


def _resident_layer(stacked_shape, layer):
    shape = tuple(stacked_shape[1:])
    nd = len(shape)
    return pl.BlockSpec((None,) + shape, lambda *_: (layer,) + (0,) * nd,
                        pipeline_mode=pl.Buffered(1))


def _rms(x, g):
    return x * lax.rsqrt(jnp.mean(x * x, axis=-1, keepdims=True) + NORM_EPS) * g


def _sigmoid(x):
    return 1.0 / (1.0 + jnp.exp(-x))


def _gelu_tanh(x):
    return x * (0.5 * (1.0 + jnp.tanh(0.7978845608028654 * (x + 0.044715 * (x * x * x)))))


def _dot(a, b):
    return jnp.dot(a, b, preferred_element_type=F32)


def _dot_nt(a, b):
    return lax.dot_general(a, b, (((1,), (1,)), ((), ())), preferred_element_type=F32)


def _dot_tn(a, b):
    return lax.dot_general(a, b, (((0,), (0,)), ((), ())), preferred_element_type=F32)


def _split2(x):
    hi = x.astype(BF16)
    lo = (x - hi.astype(F32)).astype(BF16)
    return hi, lo


def _split3(x):
    hi = x.astype(BF16)
    r = x - hi.astype(F32)
    mid = r.astype(BF16)
    lo = (r - mid.astype(F32)).astype(BF16)
    return hi, mid, lo


def _ffn_kernel(h_ref, pre_ref, wgu_ref, wd_ref, post_ref, o_ref, acc_ref):
    x = h_ref[...]
    xn = _rms(x, pre_ref[...]).astype(BF16)
    hidden = wd_ref.shape[0]
    fc = FFN_HIDDEN_CHUNK
    for c in range(hidden // fc):
        g = _dot(xn, wgu_ref[:, c * fc:(c + 1) * fc])
        u = _dot(xn, wgu_ref[:, hidden + c * fc:hidden + (c + 1) * fc])
        a = (g * _sigmoid(g) * u).astype(BF16)
        d = _dot(a, wd_ref[c * fc:(c + 1) * fc, :])
        if c == 0:
            acc_ref[...] = d
        else:
            acc_ref[...] += d
    o_ref[...] = x + 0.5 * _rms(acc_ref[...], post_ref[...])


def _ffn(h, pre_g, w_gu, w_down, post_g, layer):
    t, d = h.shape
    hidden = w_down.shape[1]
    assert hidden % FFN_HIDDEN_CHUNK == 0 and t % FFN_TOKENS == 0
    tm = FFN_TOKENS
    return pl.pallas_call(
        _ffn_kernel,
        out_shape=jax.ShapeDtypeStruct((t, d), F32),
        grid=(t // tm,),
        in_specs=[
            pl.BlockSpec((tm, d), lambda i: (i, 0)),
            _resident((1, d)),
            _resident_layer(w_gu.shape, layer),
            _resident_layer(w_down.shape, layer),
            _resident((1, d)),
        ],
        out_specs=pl.BlockSpec((tm, d), lambda i: (i, 0)),
        scratch_shapes=[pltpu.VMEM((tm, d), F32)],
        compiler_params=pltpu.CompilerParams(
            dimension_semantics=("arbitrary",), vmem_limit_bytes=VMEM_LIMIT),
        name="ffn",
    )(h, pre_g.reshape(1, d), w_gu, w_down, post_g.reshape(1, d))


def _inproj_layout(d_model):
    aw = ATTN_HEADS * ATTN_HEAD_DIM
    sw = SGU_GROUPS * SGU_GROUP_DIM
    return (("q", aw, aw), ("k", aw, aw), ("v", aw, aw), ("z", SSM_WIDTH, SSM_WIDTH),
            ("xbc", SSM_CONV_CH, SSM_CONV_CH), ("dt", SSM_HEADS, LANES),
            ("u", sw, sw), ("vg", sw, sw), ("gates", N_BRANCHES * d_model, N_BRANCHES * d_model))


def _inproj_kernel(h_ref, pre_ref, w_ref, convw_ref, convb_ref, lng_ref, lnb_ref,
                   q_ref, k_ref, v_ref, zs_ref, xc_ref, dt_ref, ua_ref, vn_ref, gs_ref,
                   xraw_ref, va_ref, xn_ref, *, tiles_per_seq):
    tm = h_ref.shape[0]
    pad = SUBLANES

    @pl.when(pl.program_id(0) % tiles_per_seq == 0)
    def _():
        xraw_ref[0:pad, :] = jnp.zeros((pad, SSM_CONV_CH), F32)

    xn_ref[...] = _rms(h_ref[...], pre_ref[...]).astype(BF16)

    def store_q(cols, r):
        q_ref[:, cols] = (r * (ATTN_HEAD_DIM ** -0.5 * LOG2E)).astype(BF16)

    def store_plain(o_ref):
        def store(cols, r):
            o_ref[:, cols] = r.astype(o_ref.dtype)
        return store

    def store_silu(cols, r):
        zs_ref[:, cols] = (r * _sigmoid(r)).astype(BF16)

    def store_conv(cols, r):
        xraw_ref[pad:pad + tm, cols] = r
        conv = convb_ref[:, cols] + convw_ref[SSM_CONV - 1:SSM_CONV, cols] * r
        for kk in range(SSM_CONV - 1):
            lag = SSM_CONV - 1 - kk
            conv = conv + convw_ref[kk:kk + 1, cols] * xraw_ref[pad - lag:pad - lag + tm, cols]
        xc_ref[:, cols] = (conv * _sigmoid(conv)).astype(BF16)

    def store_gelu(cols, r):
        ua_ref[:, cols] = _gelu_tanh(r).astype(BF16)

    def store_gelu_f32(cols, r):
        va_ref[:, cols] = _gelu_tanh(r)

    def store_sigmoid(cols, r):
        gs_ref[:, cols] = _sigmoid(r).astype(BF16)

    def layer_norm_v():
        va = va_ref[...]
        mu = jnp.mean(va, axis=-1, keepdims=True)
        var = jnp.mean(jnp.square(va - mu), axis=-1, keepdims=True)
        vn = (va - mu) * lax.rsqrt(var + NORM_EPS) * lng_ref[...] + lnb_ref[...]
        vn_ref[...] = vn.astype(BF16)

    def carry_rows():
        xraw_ref[0:pad, :] = xraw_ref[tm:tm + pad, :]

    stores = dict(q=store_q, k=store_plain(k_ref), v=store_plain(v_ref), z=store_silu,
                  xbc=store_conv, dt=store_plain(dt_ref), u=store_gelu, vg=store_gelu_f32,
                  gates=store_sigmoid)
    after = dict(vg=layer_norm_v, xbc=carry_rows)
    offsets, off = {}, 0
    for name, _, width in _inproj_layout(h_ref.shape[1]):
        offsets[name] = (off, width)
        off += width
    def chunks(names):
        out = []
        for name in names:
            off, width = offsets[name]
            starts = list(range(0, width, INPROJ_COL_CHUNK))
            for c0 in starts:
                out.append((name, off, c0, min(INPROJ_COL_CHUNK, width - c0), c0 == starts[-1]))
        return out

    heavy = chunks(("vg", "xbc", "u", "z"))
    light = chunks(("q", "k", "v", "dt", "gates"))
    per_heavy = len(light) // len(heavy)
    order = []
    while heavy or light:
        order += heavy[:1] + light[:per_heavy]
        heavy, light = heavy[1:], light[per_heavy:]
    for name, off, c0, n, last in order:
        stores[name](slice(c0, c0 + n), _dot(xn_ref[...], w_ref[:, off + c0:off + c0 + n]))
        if last and name in after:
            after[name]()


def _inproj(h, pre_g, w_pad, conv_w, conv_b, ln_g, ln_b, seq_len, layer):
    t, d = h.shape
    tm = INPROJ_TOKENS
    layout = _inproj_layout(d)
    sw = SGU_GROUPS * SGU_GROUP_DIM
    assert seq_len % tm == 0 and t % seq_len == 0
    assert w_pad.shape[2] == sum(p for _, _, p in layout)
    out_shape = tuple(jax.ShapeDtypeStruct((t, p), F32 if name == "dt" else BF16)
                      for name, _, p in layout)
    out_specs = tuple(pl.BlockSpec((tm, p), lambda i: (i, 0)) for _, _, p in layout)
    return pl.pallas_call(
        functools.partial(_inproj_kernel, tiles_per_seq=seq_len // tm),
        out_shape=out_shape,
        grid=(t // tm,),
        in_specs=[
            pl.BlockSpec((tm, d), lambda i: (i, 0)),
            _resident((1, d)),
            _resident_layer(w_pad.shape, layer),
            _resident((SSM_CONV, SSM_CONV_CH)), _resident((1, SSM_CONV_CH)),
            _resident((1, sw)), _resident((1, sw)),
        ],
        out_specs=out_specs,
        scratch_shapes=[
            pltpu.VMEM((tm + SUBLANES, SSM_CONV_CH), F32),
            pltpu.VMEM((tm, sw), F32),
            pltpu.VMEM((tm, d), BF16),
        ],
        compiler_params=pltpu.CompilerParams(
            dimension_semantics=("arbitrary",), vmem_limit_bytes=VMEM_LIMIT),
        name="inproj",
    )(h, pre_g.reshape(1, d), w_pad, conv_w, conv_b.reshape(1, -1), ln_g.reshape(1, -1),
      ln_b.reshape(1, -1))


def _pad_w_in(w_in):
    depth, d, cols = w_in.shape
    pieces, off, start = [], 0, 0
    for _, width, padded in _inproj_layout(d):
        off += width
        if padded > width:
            pieces += [w_in[:, :, start:off], jnp.zeros((depth, d, padded - width), w_in.dtype)]
            start = off
    assert off == cols
    pieces.append(w_in[:, :, start:])
    return jnp.concatenate(pieces, axis=2).astype(BF16)


def _attn_kernel(slope_ref, q_ref, k_ref, v_ref, o_ref, vt_ref, kmean_ref, sel_ref, bias_ref,
                 s_ref, cmax_ref, p_ref, m_ref, acc_ref):
    blk = MOBA_BLOCK
    dh = ATTN_HEAD_DIM
    seq = q_ref.shape[1]
    nb = seq // blk
    gk = bias_ref.shape[0]
    group = gk // blk
    slope = slope_ref[0, 0:1, :] * LOG2E
    slope_row = jnp.concatenate([slope] * (blk // LANES), axis=1)
    bias_ref[...] = lax.broadcasted_iota(jnp.int32, (gk, blk), 0).astype(F32) * slope_row
    m_ref[...] = jnp.full(m_ref.shape, NEG_BIG, F32)
    acc_ref[...] = jnp.zeros_like(acc_ref)
    extra = lax.broadcasted_iota(jnp.int32, (vt_ref.shape[0] - dh, seq), 0)
    vt_ref[dh:, :] = jnp.where(extra == 0, 1.0, 0.0).astype(BF16)

    def prologue(jb, carry):
        off = pl.multiple_of(jb * blk, blk)
        kb = k_ref[0, pl.ds(off, blk), :].astype(F32)
        kmean_ref[pl.ds(jb, 1), :] = jnp.sum(kb, axis=0, keepdims=True) * (1.0 / blk)
        vb = v_ref[0, pl.ds(off, blk), :].astype(F32)
        vt_ref[0:dh, pl.ds(off, blk)] = vb.T.astype(BF16)
        return carry

    lax.fori_loop(0, nb, prologue, 0, unroll=2)

    km_hi, km_lo = _split2(kmean_ref[...])
    q_all = q_ref[0]
    gate = _dot_nt(km_hi, q_all) + _dot_nt(km_lo, q_all)
    blk_ids = lax.broadcasted_iota(jnp.int32, (nb, seq), 0)
    own_blk = lax.shift_right_logical(lax.broadcasted_iota(jnp.int32, (nb, seq), 1),
                                      blk.bit_length() - 1)
    gate = jnp.where(blk_ids < own_blk, gate, -jnp.inf)
    sel = jnp.where(blk_ids == own_blk, 1.0, 0.0)
    blk_ids = blk_ids.astype(F32)
    for _ in range(MOBA_TOPK):
        mx = jnp.max(gate, axis=0, keepdims=True)
        first = jnp.min(jnp.where(gate == mx, blk_ids, float(nb)), axis=0, keepdims=True)
        hit = blk_ids == first
        sel = jnp.where(jnp.logical_and(hit, mx > -jnp.inf), 1.0, sel)
        gate = jnp.where(hit, -jnp.inf, gate)
    sel_ref[...] = sel

    key_minus_qry = (lax.broadcasted_iota(jnp.int32, (blk, blk), 0)
                     - lax.broadcasted_iota(jnp.int32, (blk, blk), 1))

    def score_tile(g, step, slot, nblocks=group, own_block=None):
        r0 = pl.multiple_of(g * gk, gk)
        i = jnp.minimum(step, nb - 1)
        qi = q_ref[0, pl.ds(pl.multiple_of(i * blk, blk), blk), :]
        cms = []
        for u in range(nblocks):
            rows = slice(u * blk, (u + 1) * blk)
            s = _dot_nt(k_ref[0, pl.ds(r0 + u * blk, blk), :], qi) + bias_ref[rows, :]
            if u == own_block:
                s = jnp.where(key_minus_qry <= 0, s, NEG_BIG)
            s_ref[slot, rows, :] = s
            cms.append(jnp.max(s, axis=0, keepdims=True))
        cms += [jnp.full((1, blk), NEG_BIG, F32)] * (group - nblocks)
        cmax_ref[slot] = jnp.concatenate(cms, axis=0)

    def softmax_tile(g, i, slot, nblocks=group):
        r0 = pl.multiple_of(g * gk, gk)
        off = slope_row * jnp.asarray(g * gk - i * blk, F32)
        chosen = sel_ref[pl.ds(pl.multiple_of(g * group, group), group),
                         pl.ds(pl.multiple_of(i * blk, blk), blk)] > 0.0
        m_old = m_ref[pl.ds(i, 1), :]
        m_new = jnp.maximum(m_old, jnp.max(jnp.where(chosen, cmax_ref[slot] + off, NEG_BIG),
                                           axis=0, keepdims=True))
        base = m_new - off
        for u in range(nblocks):
            shift = jnp.where(chosen[u:u + 1, :], base, POS_BIG)
            x = (s_ref[slot, u * blk:(u + 1) * blk, :] - shift).astype(BF16)
            p_ref[u * blk:(u + 1) * blk, :] = jnp.exp2(x)
        pv = _dot(vt_ref[:, pl.ds(r0, nblocks * blk)], p_ref[0:nblocks * blk, :])
        acc_ref[i] = jnp.exp2(m_old - m_new) * acc_ref[i] + pv
        m_ref[pl.ds(i, 1), :] = m_new

    def key_group(g, carry):
        first = g * group
        score_tile(g, first, 0, 1, 0)
        for r in range(group):
            if r + 1 < group:
                score_tile(g, first + r + 1, (r + 1) % 2, r + 2, r + 1)
            else:
                score_tile(g, first + group, group % 2)
            softmax_tile(g, first + r, r % 2, r + 1)

        @pl.when(first + group < nb)
        def _():
            score_tile(g, first + group + 1, 1)

            def tile_pair(i, score_ahead=True):
                for slot in range(2):
                    softmax_tile(g, i + slot, slot)
                    if score_ahead:
                        score_tile(g, i + 2 + slot, slot)

            def later_tiles(t, c):
                tile_pair(first + 2 * t)
                return c

            lax.fori_loop(group // 2, (nb - first) // 2 - 1, later_tiles, 0)
            tile_pair(nb - 2, score_ahead=False)

        return carry

    lax.fori_loop(0, nb // group, key_group, 0)

    def finalize(i, carry):
        a = acc_ref[i]
        out = a[0:dh, :] / a[dh:dh + 1, :]
        o_ref[0, pl.ds(pl.multiple_of(i * blk, blk), blk), :] = out.T.astype(o_ref.dtype)
        return carry

    lax.fori_loop(0, nb, finalize, 0, unroll=2)


def _attention(q, k, v):
    b, s, w = q.shape
    dh = ATTN_HEAD_DIM
    nh = w // dh
    assert s % MOBA_BLOCK == 0 and s // MOBA_BLOCK >= MOBA_TOPK and dh == LANES
    nb = s // MOBA_BLOCK
    group = min(ATTN_GROUP, nb)
    assert nb % group == 0 and group % 2 == 0
    gk = group * MOBA_BLOCK
    slopes = 2.0 **(-8.0 * (jnp.arange(nh, dtype=F32) + 1.0) / nh)
    slopes = jnp.broadcast_to(slopes[:, None, None], (nh, SUBLANES, LANES))
    head_spec = pl.BlockSpec((1, s, dh), lambda bi, hi: (bi, 0, hi))
    return pl.pallas_call(
        _attn_kernel,
        out_shape=jax.ShapeDtypeStruct((b, s, w), BF16),
        grid=(b, nh),
        in_specs=[pl.BlockSpec((1, SUBLANES, LANES), lambda bi, hi: (hi, 0, 0)),
                  head_spec, head_spec, head_spec],
        out_specs=head_spec,
        scratch_shapes=[
            pltpu.VMEM((dh + BF16_ROWS, s), BF16),
            pltpu.VMEM((nb, dh), F32),
            pltpu.VMEM((nb, s), F32),
            pltpu.VMEM((gk, MOBA_BLOCK), F32),
            pltpu.VMEM((2, gk, MOBA_BLOCK), F32),
            pltpu.VMEM((2, group, MOBA_BLOCK), F32),
            pltpu.VMEM((gk, MOBA_BLOCK), BF16),
            pltpu.VMEM((nb, MOBA_BLOCK), F32),
            pltpu.VMEM((nb, dh + BF16_ROWS, MOBA_BLOCK), F32),
        ],
        compiler_params=pltpu.CompilerParams(
            dimension_semantics=("arbitrary", "arbitrary"), vmem_limit_bytes=VMEM_LIMIT),
        name="moba_attention",
    )(slopes, q, k, v)


def _ssm_kernel(xc_ref, dt_ref, zs_ref, dtb_ref, alog_ref, dskip_ref,
                normg_ref, expand_ref, o_ref, state_ref, y_ref):
    ts = xc_ref.shape[1]
    q = SSM_CHUNK
    hp = SSM_HEAD_DIM
    gw = SSM_WIDTH // SSM_GROUPS
    hpg = SSM_HEADS // SSM_GROUPS

    @pl.when(pl.program_id(1) == 0)
    def _():
        state_ref[...] = jnp.zeros_like(state_ref)

    a_neg = -jnp.exp(alog_ref[...])
    expand = expand_ref[...]
    row = lax.broadcasted_iota(jnp.int32, (q, q), 0)
    col = lax.broadcasted_iota(jnp.int32, (q, q), 1)
    lower = row >= col
    tri = jnp.where(lower, 1.0, 0.0).astype(BF16)

    def expand_heads(w):
        hi, lo = _split2(w)
        return _dot(hi, expand) + _dot(lo, expand)

    def chunk(c, carry):
        r0 = pl.multiple_of(c * q, q)
        rows = pl.ds(r0, q)
        xs_b = xc_ref[0, rows, 0:SSM_WIDTH]
        xs = xs_b.astype(F32)
        dtr = dt_ref[0, rows, :] + dtb_ref[...]
        dt = jnp.maximum(dtr, 0.0) + jnp.log1p(jnp.exp(-jnp.abs(dtr)))
        da = dt * a_neg
        d_hi, d_mid, d_lo = _split3(da)
        a_cs = _dot(tri, d_hi) + _dot(tri, d_mid) + _dot(tri, d_lo)
        a_cs_t = a_cs.T
        dt_t = dt.T
        a_last = a_cs[q - 1:q, :]

        ea_x = expand_heads(jnp.exp(a_cs))
        wend_x = expand_heads(jnp.exp(a_last - a_cs) * dt)
        cdec_x = ea_x[q - 1:q, :]
        xs_w = (xs * wend_x).astype(BF16)

        for g in range(SSM_GROUPS):
            bg = xc_ref[0, rows, SSM_WIDTH + g * SSM_STATE:SSM_WIDTH + (g + 1) * SSM_STATE]
            cg = xc_ref[0, rows, SSM_WIDTH + SSM_BC + g * SSM_STATE:
                        SSM_WIDTH + SSM_BC + (g + 1) * SSM_STATE]
            cb = _dot_nt(cg, bg)
            lanes = slice(g * gw, (g + 1) * gw)
            st = state_ref[:, lanes]
            y_ref[:, lanes] = _dot(cg, st.astype(BF16)) * ea_x[:, lanes]
            state_ref[:, lanes] = st * cdec_x[:, lanes] + _dot_tn(bg, xs_w[:, lanes])
            for hh in range(0, hpg, 2):
                pair = []
                for h in (g * hpg + hh, g * hpg + hh + 1):
                    seg = a_cs[:, h:h + 1] - a_cs_t[h:h + 1, :]
                    wts = cb * jnp.where(lower, jnp.exp(seg), 0.0) * dt_t[h:h + 1, :]
                    pair.append(_dot(wts.astype(BF16), xs_b[:, h * hp:(h + 1) * hp]))
                h0 = (g * hpg + hh) * hp
                y_ref[:, h0:h0 + 2 * hp] += jnp.concatenate(pair, axis=1)

        y = (y_ref[...] + dskip_ref[...] * xs) * zs_ref[0, rows, :].astype(F32)
        o_ref[0, rows, :] = _rms(y, normg_ref[...]).astype(o_ref.dtype)
        return carry

    lax.fori_loop(0, ts // q, chunk, 0, unroll=4)


def _ssm(xc, dt_raw, zs, dt_bias, a_log, d_skip, norm_g):
    b, s, _ = xc.shape
    ts = min(SSM_TOKENS, s)
    assert s % ts == 0 and ts % SSM_CHUNK == 0 and SSM_HEADS <= LANES

    def lane_pad(x):
        return jnp.zeros((1, LANES), F32).at[0, :SSM_HEADS].set(x)

    head_of_lane = jnp.arange(SSM_WIDTH) // SSM_HEAD_DIM
    expand = (jnp.arange(LANES)[:, None] == head_of_lane[None, :]).astype(BF16)
    tile = lambda w: pl.BlockSpec((1, ts, w), lambda bi, ti: (bi, ti, 0))
    return pl.pallas_call(
        _ssm_kernel,
        out_shape=jax.ShapeDtypeStruct((b, s, SSM_WIDTH), BF16),
        grid=(b, s // ts),
        in_specs=[
            tile(SSM_CONV_CH), tile(LANES), tile(SSM_WIDTH),
            _resident((1, LANES)), _resident((1, LANES)),
            _resident((1, SSM_WIDTH)), _resident((1, SSM_WIDTH)),
            _resident((LANES, SSM_WIDTH)),
        ],
        out_specs=tile(SSM_WIDTH),
        scratch_shapes=[
            pltpu.VMEM((SSM_STATE, SSM_WIDTH), F32),
            pltpu.VMEM((SSM_CHUNK, SSM_WIDTH), F32),
        ],
        compiler_params=pltpu.CompilerParams(
            dimension_semantics=("arbitrary", "arbitrary"), vmem_limit_bytes=VMEM_LIMIT),
        name="ssd",
    )(xc, dt_raw, zs, lane_pad(dt_bias), lane_pad(a_log),
      jnp.repeat(d_skip, SSM_HEAD_DIM).reshape(1, -1), norm_g.reshape(1, -1), expand)


def _merge_kernel(h_ref, ya_ref, ys_ref, ua_ref, vn_ref, gs_ref, sw_ref,
                  sbt_ref, pa_ref, ps_ref, pc_ref, wo_ref, post_ref, o_ref, yc_ref):
    tm, d = h_ref.shape
    ch, gd = SGU_CHUNK, SGU_GROUP_DIM

    row = lax.broadcasted_iota(jnp.int32, (ch, ch), 0)
    col = lax.broadcasted_iota(jnp.int32, (ch, ch), 1)
    for g in range(SGU_GROUPS):
        w_sp = jnp.where(row >= col, sw_ref[g], 0.0).astype(BF16)
        bias = sbt_ref[:, g:g + 1]
        for c in range(tm // ch):
            rs, cs = slice(c * ch, (c + 1) * ch), slice(g * gd, (g + 1) * gd)
            sv = _dot(w_sp, vn_ref[rs, cs]) + bias
            yc_ref[rs, cs] = (ua_ref[rs, cs].astype(F32) * sv).astype(BF16)

    merged = gs_ref[:, 0:d].astype(F32) * _dot(ya_ref[...], pa_ref[...])
    merged += gs_ref[:, d:2 * d].astype(F32) * _dot(ys_ref[...], ps_ref[...])
    merged += gs_ref[:, 2 * d:3 * d].astype(F32) * _dot(yc_ref[...], pc_ref[...])
    m = _dot(merged.astype(BF16), wo_ref[...])
    o_ref[...] = h_ref[...] + _rms(m, post_ref[...])


def _merge(h, ya, ys, ua, vn, gs, sgu_w, sgu_b, p_attn, p_ssm, p_sgu, w_out, post_g, layer):
    t, d = h.shape
    tm = MERGE_TOKENS
    sw = SGU_GROUPS * SGU_GROUP_DIM
    assert t % tm == 0 and tm % SGU_CHUNK == 0
    tile = lambda w: pl.BlockSpec((tm, w), lambda i: (i, 0))
    return pl.pallas_call(
        _merge_kernel,
        out_shape=jax.ShapeDtypeStruct((t, d), F32),
        grid=(t // tm,),
        in_specs=[
            tile(d), tile(ya.shape[1]), tile(ys.shape[1]), tile(sw), tile(sw), tile(N_BRANCHES * d),
            _resident(sgu_w.shape), _resident((SGU_CHUNK, SGU_GROUPS)),
            _resident_layer(p_attn.shape, layer), _resident_layer(p_ssm.shape, layer),
            _resident_layer(p_sgu.shape, layer), _resident_layer(w_out.shape, layer),
            _resident((1, d)),
        ],
        out_specs=tile(d),
        scratch_shapes=[pltpu.VMEM((tm, sw), BF16)],
        compiler_params=pltpu.CompilerParams(
            dimension_semantics=("arbitrary",), vmem_limit_bytes=VMEM_LIMIT),
        name="merge",
    )(h, ya, ys, ua, vn, gs, sgu_w, sgu_b.T, p_attn, p_ssm, p_sgu, w_out, post_g.reshape(1, d))


def kernel(x, ffn1_pre_g, ffn1_w_gu, ffn1_w_down, ffn1_post_g, mix_pre_g, w_in, conv_w, conv_b, dt_bias, a_log, d_skip, ssm_norm_g, sgu_ln_g, sgu_ln_b, sgu_w, sgu_b, p_attn, p_ssm, p_sgu, w_out, mix_post_g, ffn2_pre_g, ffn2_w_gu, ffn2_w_down, ffn2_post_g):
    b, s, d = x.shape
    depth = w_in.shape[0]
    h = x.reshape(b * s, d)
    as_b = lambda a: a.astype(BF16)
    ffn1_w_gu, ffn1_w_down, ffn2_w_gu, ffn2_w_down = map(
        as_b, (ffn1_w_gu, ffn1_w_down, ffn2_w_gu, ffn2_w_down))
    p_attn, p_ssm, p_sgu, w_out = map(as_b, (p_attn, p_ssm, p_sgu, w_out))
    w_pad = _pad_w_in(w_in)
    for i in range(depth):
        h = _ffn(h, ffn1_pre_g[i], ffn1_w_gu, ffn1_w_down, ffn1_post_g[i], i)
        q, k, v, zs, xc, dt_raw, ua, vn, gs = _inproj(
            h, mix_pre_g[i], w_pad, conv_w[i], conv_b[i], sgu_ln_g[i], sgu_ln_b[i], s, i)
        seq = lambda a: a.reshape(b, s, a.shape[-1])
        ya = _attention(seq(q), seq(k), seq(v))
        ys = _ssm(seq(xc), seq(dt_raw), seq(zs), dt_bias[i], a_log[i], d_skip[i], ssm_norm_g[i])
        h = _merge(h, ya.reshape(b * s, -1), ys.reshape(b * s, -1), ua, vn, gs,
                   sgu_w[i], sgu_b[i], p_attn, p_ssm, p_sgu, w_out, mix_post_g[i], i)
        h = _ffn(h, ffn2_pre_g[i], ffn2_w_gu, ffn2_w_down, ffn2_post_g[i], i)
    return h.reshape(b, s, d)
```

```python
import functools

import jax
import jax.numpy as jnp
from jax import lax
from jax.experimental import pallas as pl
from jax.experimental.pallas import tpu as pltpu

F32 = jnp.float32
BF16 = jnp.bfloat16

NORM_EPS = 1e-6
ATTN_HEADS = 8
ATTN_HEAD_DIM = 128
MOBA_BLOCK = 256
MOBA_TOPK = 3
SSM_HEADS = 16
SSM_HEAD_DIM = 64
SSM_WIDTH = SSM_HEADS * SSM_HEAD_DIM
SSM_GROUPS = 2
SSM_STATE = 128
SSM_CONV = 4
SSM_CHUNK = 128
SSM_BC = SSM_GROUPS * SSM_STATE
SSM_CONV_CH = SSM_WIDTH + 2 * SSM_BC
SGU_GROUPS = 8
SGU_GROUP_DIM = 128
SGU_CHUNK = 128
N_BRANCHES = 3

LANES = 128
SUBLANES = 8
BF16_ROWS = 16
LOG2E = 1.4426950408889634
VMEM_LIMIT = 56 * 1024 * 1024
NEG_BIG = -1e30
POS_BIG = 1e30

FFN_TOKENS = 512
FFN_HIDDEN_CHUNK = 256
INPROJ_TOKENS = 256
INPROJ_COL_CHUNK = 256
SSM_TOKENS = 512
MERGE_TOKENS = 512
ATTN_GROUP = 8
def _resident(shape):
    nd = len(shape)
    return pl.BlockSpec(shape, lambda *_: (0,) * nd, pipeline_mode=pl.Buffered(1))


def _resident_layer(stacked_shape, layer):
    shape = tuple(stacked_shape[1:])
    nd = len(shape)
    return pl.BlockSpec((None,) + shape, lambda *_: (layer,) + (0,) * nd,
                        pipeline_mode=pl.Buffered(1))


def _rms(x, g):
    return x * lax.rsqrt(jnp.mean(x * x, axis=-1, keepdims=True) + NORM_EPS) * g


def _sigmoid(x):
    return 1.0 / (1.0 + jnp.exp(-x))


def _gelu_tanh(x):
    return x * (0.5 * (1.0 + jnp.tanh(0.7978845608028654 * (x + 0.044715 * (x * x * x)))))


def _dot(a, b):
    return jnp.dot(a, b, preferred_element_type=F32)


def _dot_nt(a, b):
    return lax.dot_general(a, b, (((1,), (1,)), ((), ())), preferred_element_type=F32)


def _dot_tn(a, b):
    return lax.dot_general(a, b, (((0,), (0,)), ((), ())), preferred_element_type=F32)


def _split2(x):
    hi = x.astype(BF16)
    lo = (x - hi.astype(F32)).astype(BF16)
    return hi, lo


def _split3(x):
    hi = x.astype(BF16)
    r = x - hi.astype(F32)
    mid = r.astype(BF16)
    lo = (r - mid.astype(F32)).astype(BF16)
    return hi, mid, lo


def _ffn_kernel(h_ref, pre_ref, wgu_ref, wd_ref, post_ref, o_ref, acc_ref):
    x = h_ref[...]
    xn = _rms(x, pre_ref[...]).astype(BF16)
    hidden = wd_ref.shape[0]
    fc = FFN_HIDDEN_CHUNK
    for c in range(hidden // fc):
        g = _dot(xn, wgu_ref[:, c * fc:(c + 1) * fc])
        u = _dot(xn, wgu_ref[:, hidden + c * fc:hidden + (c + 1) * fc])
        a = (g * _sigmoid(g) * u).astype(BF16)
        d = _dot(a, wd_ref[c * fc:(c + 1) * fc, :])
        if c == 0:
            acc_ref[...] = d
        else:
            acc_ref[...] += d
    o_ref[...] = x + 0.5 * _rms(acc_ref[...], post_ref[...])


def _ffn(h, pre_g, w_gu, w_down, post_g, layer):
    t, d = h.shape
    hidden = w_down.shape[1]
    assert hidden % FFN_HIDDEN_CHUNK == 0 and t % FFN_TOKENS == 0
    tm = FFN_TOKENS
    return pl.pallas_call(
        _ffn_kernel,
        out_shape=jax.ShapeDtypeStruct((t, d), F32),
        grid=(t // tm,),
        in_specs=[
            pl.BlockSpec((tm, d), lambda i: (i, 0)),
            _resident((1, d)),
            _resident_layer(w_gu.shape, layer),
            _resident_layer(w_down.shape, layer),
            _resident((1, d)),
        ],
        out_specs=pl.BlockSpec((tm, d), lambda i: (i, 0)),
        scratch_shapes=[pltpu.VMEM((tm, d), F32)],
        compiler_params=pltpu.CompilerParams(
            dimension_semantics=("arbitrary",), vmem_limit_bytes=VMEM_LIMIT),
        name="ffn",
    )(h, pre_g.reshape(1, d), w_gu, w_down, post_g.reshape(1, d))


def _inproj_layout(d_model):
    aw = ATTN_HEADS * ATTN_HEAD_DIM
    sw = SGU_GROUPS * SGU_GROUP_DIM
    return (("q", aw, aw), ("k", aw, aw), ("v", aw, aw), ("z", SSM_WIDTH, SSM_WIDTH),
            ("xbc", SSM_CONV_CH, SSM_CONV_CH), ("dt", SSM_HEADS, LANES),
            ("u", sw, sw), ("vg", sw, sw), ("gates", N_BRANCHES * d_model, N_BRANCHES * d_model))


def _inproj_kernel(*refs, tiles_per_seq, n_pieces):
    h_ref, pre_ref = refs[:2]
    w_refs = refs[2:2 + n_pieces]
    (convw_ref, convb_ref, lng_ref, lnb_ref,
     q_ref, k_ref, v_ref, zs_ref, xc_ref, dt_ref, ua_ref, vn_ref, gs_ref,
     xraw_ref, va_ref, xn_ref) = refs[2 + n_pieces:]
    tm = h_ref.shape[0]
    pad = SUBLANES

    @pl.when(pl.program_id(0) % tiles_per_seq == 0)
    def _():
        xraw_ref[0:pad, :] = jnp.zeros((pad, SSM_CONV_CH), F32)

    xn_ref[...] = _rms(h_ref[...], pre_ref[...]).astype(BF16)

    def store_q(cols, r):
        q_ref[:, cols] = (r * (ATTN_HEAD_DIM ** -0.5 * LOG2E)).astype(BF16)

    def store_plain(o_ref):
        def store(cols, r):
            o_ref[:, cols] = r.astype(o_ref.dtype)
        return store

    def store_silu(cols, r):
        zs_ref[:, cols] = (r * _sigmoid(r)).astype(BF16)

    def store_conv(cols, r):
        xraw_ref[pad:pad + tm, cols] = r
        conv = convb_ref[:, cols] + convw_ref[SSM_CONV - 1:SSM_CONV, cols] * r
        for kk in range(SSM_CONV - 1):
            lag = SSM_CONV - 1 - kk
            conv = conv + convw_ref[kk:kk + 1, cols] * xraw_ref[pad - lag:pad - lag + tm, cols]
        xc_ref[:, cols] = (conv * _sigmoid(conv)).astype(BF16)

    def store_gelu(cols, r):
        ua_ref[:, cols] = _gelu_tanh(r).astype(BF16)

    def store_gelu_f32(cols, r):
        va_ref[:, cols] = _gelu_tanh(r)

    def store_sigmoid(cols, r):
        gs_ref[:, cols] = _sigmoid(r).astype(BF16)

    def layer_norm_v():
        va = va_ref[...]
        mu = jnp.mean(va, axis=-1, keepdims=True)
        var = jnp.mean(jnp.square(va - mu), axis=-1, keepdims=True)
        vn = (va - mu) * lax.rsqrt(var + NORM_EPS) * lng_ref[...] + lnb_ref[...]
        vn_ref[...] = vn.astype(BF16)

    def carry_rows():
        xraw_ref[0:pad, :] = xraw_ref[tm:tm + pad, :]

    stores = dict(q=store_q, k=store_plain(k_ref), v=store_plain(v_ref), z=store_silu,
                  xbc=store_conv, dt=store_plain(dt_ref), u=store_gelu, vg=store_gelu_f32,
                  gates=store_sigmoid)
    after = dict(vg=layer_norm_v, xbc=carry_rows)
    offsets, piece, off = {}, 0, 0
    for name, _, width in _inproj_layout(h_ref.shape[1]):
        if off == w_refs[piece].shape[1]:
            piece, off = piece + 1, 0
        offsets[name] = (w_refs[piece], off, width)
        off += width

    def chunks(names):
        out = []
        for name in names:
            w_ref, off, width = offsets[name]
            starts = list(range(0, width, INPROJ_COL_CHUNK))
            for c0 in starts:
                n = min(INPROJ_COL_CHUNK, width - c0)
                out.append((name, w_ref, off + c0, c0, n, c0 == starts[-1]))
        return out

    heavy = chunks(("vg", "xbc", "u", "z"))
    light = chunks(("q", "k", "v", "dt", "gates"))
    per_heavy = len(light) // len(heavy)
    order = []
    while heavy or light:
        order += heavy[:1] + light[:per_heavy]
        heavy, light = heavy[1:], light[per_heavy:]
    for name, w_ref, w0, c0, n, last in order:
        stores[name](slice(c0, c0 + n), _dot(xn_ref[...], w_ref[:, w0:w0 + n]))
        if last and name in after:
            after[name]()


def _inproj(h, pre_g, w_pieces, conv_w, conv_b, ln_g, ln_b, seq_len, layer):
    t, d = h.shape
    tm = INPROJ_TOKENS
    layout = _inproj_layout(d)
    sw = SGU_GROUPS * SGU_GROUP_DIM
    assert seq_len % tm == 0 and t % seq_len == 0
    assert sum(w.shape[2] for w in w_pieces) == sum(p for _, _, p in layout)
    out_shape = tuple(jax.ShapeDtypeStruct((t, p), F32 if name == "dt" else BF16)
                      for name, _, p in layout)
    out_specs = tuple(pl.BlockSpec((tm, p), lambda i: (i, 0)) for _, _, p in layout)
    return pl.pallas_call(
        functools.partial(_inproj_kernel, tiles_per_seq=seq_len // tm, n_pieces=len(w_pieces)),
        out_shape=out_shape,
        grid=(t // tm,),
        in_specs=[
            pl.BlockSpec((tm, d), lambda i: (i, 0)),
            _resident((1, d)),
            *[_resident_layer(w.shape, layer) for w in w_pieces],
            _resident((SSM_CONV, SSM_CONV_CH)), _resident((1, SSM_CONV_CH)),
            _resident((1, sw)), _resident((1, sw)),
        ],
        out_specs=out_specs,
        scratch_shapes=[
            pltpu.VMEM((tm + SUBLANES, SSM_CONV_CH), F32),
            pltpu.VMEM((tm, sw), F32),
            pltpu.VMEM((tm, d), BF16),
        ],
        compiler_params=pltpu.CompilerParams(
            dimension_semantics=("arbitrary",), vmem_limit_bytes=VMEM_LIMIT),
        name="inproj",
    )(h, pre_g.reshape(1, d), *w_pieces, conv_w, conv_b.reshape(1, -1), ln_g.reshape(1, -1),
      ln_b.reshape(1, -1))


def _inproj_pieces(d_model):
    runs, start, off = [], 0, 0
    for _, width, padded in _inproj_layout(d_model):
        if padded > width:
            if off > start:
                runs.append((start, off, off - start))
            runs.append((off, off + width, padded))
            start = off + width
        off += width
    if off > start:
        runs.append((start, off, off - start))
    return runs


def _split_w_in(w_in):
    depth, d, cols = w_in.shape
    out = []
    for start, stop, padded in _inproj_pieces(d):
        piece = w_in[:, :, start:stop].astype(BF16)
        if padded > stop - start:
            piece = jnp.pad(piece, ((0, 0), (0, 0), (0, padded - (stop - start))))
        out.append(piece)
    assert stop == cols
    return tuple(out)


def _attn_kernel(slope_ref, q_ref, k_ref, v_ref, o_ref, vt_ref, kmean_ref, sel_ref, bias_ref,
                 s_ref, cmax_ref, p_ref, m_ref, acc_ref):
    blk = MOBA_BLOCK
    dh = ATTN_HEAD_DIM
    seq = q_ref.shape[1]
    nb = seq // blk
    gk = bias_ref.shape[0]
    group = gk // blk
    slope = slope_ref[0, 0:1, :] * LOG2E
    slope_row = jnp.concatenate([slope] * (blk // LANES), axis=1)
    bias_ref[...] = lax.broadcasted_iota(jnp.int32, (gk, blk), 0).astype(F32) * slope_row
    m_ref[...] = jnp.full(m_ref.shape, NEG_BIG, F32)
    acc_ref[...] = jnp.zeros_like(acc_ref)
    extra = lax.broadcasted_iota(jnp.int32, (vt_ref.shape[0] - dh, seq), 0)
    vt_ref[dh:, :] = jnp.where(extra == 0, 1.0, 0.0).astype(BF16)

    def prologue(jb, carry):
        off = pl.multiple_of(jb * blk, blk)
        kb = k_ref[0, pl.ds(off, blk), :].astype(F32)
        kmean_ref[pl.ds(jb, 1), :] = jnp.sum(kb, axis=0, keepdims=True) * (1.0 / blk)
        vb = v_ref[0, pl.ds(off, blk), :].astype(F32)
        vt_ref[0:dh, pl.ds(off, blk)] = vb.T.astype(BF16)
        return carry

    lax.fori_loop(0, nb, prologue, 0, unroll=2)

    km_hi, km_lo = _split2(kmean_ref[...])
    q_all = q_ref[0]
    gate = _dot_nt(km_hi, q_all) + _dot_nt(km_lo, q_all)
    blk_ids = lax.broadcasted_iota(jnp.int32, (nb, seq), 0)
    own_blk = lax.shift_right_logical(lax.broadcasted_iota(jnp.int32, (nb, seq), 1),
                                      blk.bit_length() - 1)
    gate = jnp.where(blk_ids < own_blk, gate, -jnp.inf)
    sel = jnp.where(blk_ids == own_blk, 1.0, 0.0)
    blk_ids = blk_ids.astype(F32)
    for _ in range(MOBA_TOPK):
        mx = jnp.max(gate, axis=0, keepdims=True)
        first = jnp.min(jnp.where(gate == mx, blk_ids, float(nb)), axis=0, keepdims=True)
        hit = blk_ids == first
        sel = jnp.where(jnp.logical_and(hit, mx > -jnp.inf), 1.0, sel)
        gate = jnp.where(hit, -jnp.inf, gate)
    sel_ref[...] = sel

    key_minus_qry = (lax.broadcasted_iota(jnp.int32, (blk, blk), 0)
                     - lax.broadcasted_iota(jnp.int32, (blk, blk), 1))

    def score_tile(g, step, slot, nblocks=group, own_block=None):
        r0 = pl.multiple_of(g * gk, gk)
        i = jnp.minimum(step, nb - 1)
        qi = q_ref[0, pl.ds(pl.multiple_of(i * blk, blk), blk), :]
        cms = []
        for u in range(nblocks):
            rows = slice(u * blk, (u + 1) * blk)
            s = _dot_nt(k_ref[0, pl.ds(r0 + u * blk, blk), :], qi) + bias_ref[rows, :]
            if u == own_block:
                s = jnp.where(key_minus_qry <= 0, s, NEG_BIG)
            s_ref[slot, rows, :] = s
            cms.append(jnp.max(s, axis=0, keepdims=True))
        cms += [jnp.full((1, blk), NEG_BIG, F32)] * (group - nblocks)
        cmax_ref[slot] = jnp.concatenate(cms, axis=0)

    def softmax_tile(g, i, slot, nblocks=group):
        r0 = pl.multiple_of(g * gk, gk)
        off = slope_row * jnp.asarray(g * gk - i * blk, F32)
        chosen = sel_ref[pl.ds(pl.multiple_of(g * group, group), group),
                         pl.ds(pl.multiple_of(i * blk, blk), blk)] > 0.0
        m_old = m_ref[pl.ds(i, 1), :]
        m_new = jnp.maximum(m_old, jnp.max(jnp.where(chosen, cmax_ref[slot] + off, NEG_BIG),
                                           axis=0, keepdims=True))
        base = m_new - off
        for u in range(nblocks):
            shift = jnp.where(chosen[u:u + 1, :], base, POS_BIG)
            x = (s_ref[slot, u * blk:(u + 1) * blk, :] - shift).astype(BF16)
            p_ref[u * blk:(u + 1) * blk, :] = jnp.exp2(x)
        pv = _dot(vt_ref[:, pl.ds(r0, nblocks * blk)], p_ref[0:nblocks * blk, :])
        acc_ref[i] = jnp.exp2(m_old - m_new) * acc_ref[i] + pv
        m_ref[pl.ds(i, 1), :] = m_new

    def key_group(g, carry):
        first = g * group
        score_tile(g, first, 0, 1, 0)
        for r in range(group):
            if r + 1 < group:
                score_tile(g, first + r + 1, (r + 1) % 2, r + 2, r + 1)
            else:
                score_tile(g, first + group, group % 2)
            softmax_tile(g, first + r, r % 2, r + 1)

        @pl.when(first + group < nb)
        def _():
            score_tile(g, first + group + 1, 1)

            def tile_pair(i, score_ahead=True):
                for slot in range(2):
                    softmax_tile(g, i + slot, slot)
                    if score_ahead:
                        score_tile(g, i + 2 + slot, slot)

            def later_tiles(t, c):
                tile_pair(first + 2 * t)
                return c

            lax.fori_loop(group // 2, (nb - first) // 2 - 1, later_tiles, 0)
            tile_pair(nb - 2, score_ahead=False)

        return carry

    lax.fori_loop(0, nb // group, key_group, 0)

    def finalize(i, carry):
        a = acc_ref[i]
        out = a[0:dh, :] / a[dh:dh + 1, :]
        o_ref[0, pl.ds(pl.multiple_of(i * blk, blk), blk), :] = out.T.astype(o_ref.dtype)
        return carry

    lax.fori_loop(0, nb, finalize, 0, unroll=2)


def _attention(q, k, v):
    b, s, w = q.shape
    dh = ATTN_HEAD_DIM
    nh = w // dh
    assert s % MOBA_BLOCK == 0 and s // MOBA_BLOCK >= MOBA_TOPK and dh == LANES
    nb = s // MOBA_BLOCK
    group = min(ATTN_GROUP, nb)
    assert nb % group == 0 and group % 2 == 0
    gk = group * MOBA_BLOCK
    slopes = 2.0 **(-8.0 * (jnp.arange(nh, dtype=F32) + 1.0) / nh)
    slopes = jnp.broadcast_to(slopes[:, None, None], (nh, SUBLANES, LANES))
    head_spec = pl.BlockSpec((1, s, dh), lambda bi, hi: (bi, 0, hi))
    return pl.pallas_call(
        _attn_kernel,
        out_shape=jax.ShapeDtypeStruct((b, s, w), BF16),
        grid=(b, nh),
        in_specs=[pl.BlockSpec((1, SUBLANES, LANES), lambda bi, hi: (hi, 0, 0)),
                  head_spec, head_spec, head_spec],
        out_specs=head_spec,
        scratch_shapes=[
            pltpu.VMEM((dh + BF16_ROWS, s), BF16),
            pltpu.VMEM((nb, dh), F32),
            pltpu.VMEM((nb, s), F32),
            pltpu.VMEM((gk, MOBA_BLOCK), F32),
            pltpu.VMEM((2, gk, MOBA_BLOCK), F32),
            pltpu.VMEM((2, group, MOBA_BLOCK), F32),
            pltpu.VMEM((gk, MOBA_BLOCK), BF16),
            pltpu.VMEM((nb, MOBA_BLOCK), F32),
            pltpu.VMEM((nb, dh + BF16_ROWS, MOBA_BLOCK), F32),
        ],
        compiler_params=pltpu.CompilerParams(
            dimension_semantics=("arbitrary", "arbitrary"), vmem_limit_bytes=VMEM_LIMIT),
        name="moba_attention",
    )(slopes, q, k, v)


def _ssm_kernel(xc_ref, dt_ref, zs_ref, dtb_ref, alog_ref, dskip_ref,
                normg_ref, expand_ref, o_ref, state_ref, y_ref):
    ts = xc_ref.shape[1]
    q = SSM_CHUNK
    hp = SSM_HEAD_DIM
    gw = SSM_WIDTH // SSM_GROUPS
    hpg = SSM_HEADS // SSM_GROUPS

    @pl.when(pl.program_id(1) == 0)
    def _():
        state_ref[...] = jnp.zeros_like(state_ref)

    a_neg = -jnp.exp(alog_ref[...])
    expand = expand_ref[...]
    row = lax.broadcasted_iota(jnp.int32, (q, q), 0)
    col = lax.broadcasted_iota(jnp.int32, (q, q), 1)
    lower = row >= col
    tri = jnp.where(lower, 1.0, 0.0).astype(BF16)

    def expand_heads(w):
        hi, lo = _split2(w)
        return _dot(hi, expand) + _dot(lo, expand)

    def chunk(c, carry):
        r0 = pl.multiple_of(c * q, q)
        rows = pl.ds(r0, q)
        xs_b = xc_ref[0, rows, 0:SSM_WIDTH]
        xs = xs_b.astype(F32)
        dtr = dt_ref[0, rows, :] + dtb_ref[...]
        dt = jnp.maximum(dtr, 0.0) + jnp.log1p(jnp.exp(-jnp.abs(dtr)))
        da = dt * a_neg
        d_hi, d_mid, d_lo = _split3(da)
        a_cs = _dot(tri, d_hi) + _dot(tri, d_mid) + _dot(tri, d_lo)
        a_cs_t = a_cs.T
        dt_t = dt.T
        a_last = a_cs[q - 1:q, :]

        ea_x = expand_heads(jnp.exp(a_cs))
        wend_x = expand_heads(jnp.exp(a_last - a_cs) * dt)
        cdec_x = ea_x[q - 1:q, :]
        xs_w = (xs * wend_x).astype(BF16)

        for g in range(SSM_GROUPS):
            bg = xc_ref[0, rows, SSM_WIDTH + g * SSM_STATE:SSM_WIDTH + (g + 1) * SSM_STATE]
            cg = xc_ref[0, rows, SSM_WIDTH + SSM_BC + g * SSM_STATE:
                        SSM_WIDTH + SSM_BC + (g + 1) * SSM_STATE]
            cb = _dot_nt(cg, bg)
            lanes = slice(g * gw, (g + 1) * gw)
            st = state_ref[:, lanes]
            y_ref[:, lanes] = _dot(cg, st.astype(BF16)) * ea_x[:, lanes]
            state_ref[:, lanes] = st * cdec_x[:, lanes] + _dot_tn(bg, xs_w[:, lanes])
            for hh in range(0, hpg, 2):
                pair = []
                for h in (g * hpg + hh, g * hpg + hh + 1):
                    seg = a_cs[:, h:h + 1] - a_cs_t[h:h + 1, :]
                    wts = cb * jnp.where(lower, jnp.exp(seg), 0.0) * dt_t[h:h + 1, :]
                    pair.append(_dot(wts.astype(BF16), xs_b[:, h * hp:(h + 1) * hp]))
                h0 = (g * hpg + hh) * hp
                y_ref[:, h0:h0 + 2 * hp] += jnp.concatenate(pair, axis=1)

        y = (y_ref[...] + dskip_ref[...] * xs) * zs_ref[0, rows, :].astype(F32)
        o_ref[0, rows, :] = _rms(y, normg_ref[...]).astype(o_ref.dtype)
        return carry

    lax.fori_loop(0, ts // q, chunk, 0, unroll=4)


def _ssm(xc, dt_raw, zs, dt_bias, a_log, d_skip, norm_g):
    b, s, _ = xc.shape
    ts = min(SSM_TOKENS, s)
    assert s % ts == 0 and ts % SSM_CHUNK == 0 and SSM_HEADS <= LANES

    def lane_pad(x):
        return jnp.zeros((1, LANES), F32).at[0, :SSM_HEADS].set(x)

    head_of_lane = jnp.arange(SSM_WIDTH) // SSM_HEAD_DIM
    expand = (jnp.arange(LANES)[:, None] == head_of_lane[None, :]).astype(BF16)
    tile = lambda w: pl.BlockSpec((1, ts, w), lambda bi, ti: (bi, ti, 0))
    return pl.pallas_call(
        _ssm_kernel,
        out_shape=jax.ShapeDtypeStruct((b, s, SSM_WIDTH), BF16),
        grid=(b, s // ts),
        in_specs=[
            tile(SSM_CONV_CH), tile(LANES), tile(SSM_WIDTH),
            _resident((1, LANES)), _resident((1, LANES)),
            _resident((1, SSM_WIDTH)), _resident((1, SSM_WIDTH)),
            _resident((LANES, SSM_WIDTH)),
        ],
        out_specs=tile(SSM_WIDTH),
        scratch_shapes=[
            pltpu.VMEM((SSM_STATE, SSM_WIDTH), F32),
            pltpu.VMEM((SSM_CHUNK, SSM_WIDTH), F32),
        ],
        compiler_params=pltpu.CompilerParams(
            dimension_semantics=("arbitrary", "arbitrary"), vmem_limit_bytes=VMEM_LIMIT),
        name="ssd",
    )(xc, dt_raw, zs, lane_pad(dt_bias), lane_pad(a_log),
      jnp.repeat(d_skip, SSM_HEAD_DIM).reshape(1, -1), norm_g.reshape(1, -1), expand)


def _merge_kernel(h_ref, ya_ref, ys_ref, ua_ref, vn_ref, gs_ref, sw_ref,
                  sbt_ref, pa_ref, ps_ref, pc_ref, wo_ref, post_ref, o_ref, yc_ref):
    tm, d = h_ref.shape
    ch, gd = SGU_CHUNK, SGU_GROUP_DIM

    row = lax.broadcasted_iota(jnp.int32, (ch, ch), 0)
    col = lax.broadcasted_iota(jnp.int32, (ch, ch), 1)
    for g in range(SGU_GROUPS):
        w_sp = jnp.where(row >= col, sw_ref[g], 0.0).astype(BF16)
        bias = sbt_ref[:, g:g + 1]
        for c in range(tm // ch):
            rs, cs = slice(c * ch, (c + 1) * ch), slice(g * gd, (g + 1) * gd)
            sv = _dot(w_sp, vn_ref[rs, cs]) + bias
            yc_ref[rs, cs] = (ua_ref[rs, cs].astype(F32) * sv).astype(BF16)

    merged = gs_ref[:, 0:d].astype(F32) * _dot(ya_ref[...], pa_ref[...])
    merged += gs_ref[:, d:2 * d].astype(F32) * _dot(ys_ref[...], ps_ref[...])
    merged += gs_ref[:, 2 * d:3 * d].astype(F32) * _dot(yc_ref[...], pc_ref[...])
    m = _dot(merged.astype(BF16), wo_ref[...])
    o_ref[...] = h_ref[...] + _rms(m, post_ref[...])


def _merge(h, ya, ys, ua, vn, gs, sgu_w, sgu_b, p_attn, p_ssm, p_sgu, w_out, post_g, layer):
    t, d = h.shape
    tm = MERGE_TOKENS
    sw = SGU_GROUPS * SGU_GROUP_DIM
    assert t % tm == 0 and tm % SGU_CHUNK == 0
    tile = lambda w: pl.BlockSpec((tm, w), lambda i: (i, 0))
    return pl.pallas_call(
        _merge_kernel,
        out_shape=jax.ShapeDtypeStruct((t, d), F32),
        grid=(t // tm,),
        in_specs=[
            tile(d), tile(ya.shape[1]), tile(ys.shape[1]), tile(sw), tile(sw), tile(N_BRANCHES * d),
            _resident(sgu_w.shape), _resident((SGU_CHUNK, SGU_GROUPS)),
            _resident_layer(p_attn.shape, layer), _resident_layer(p_ssm.shape, layer),
            _resident_layer(p_sgu.shape, layer), _resident_layer(w_out.shape, layer),
            _resident((1, d)),
        ],
        out_specs=tile(d),
        scratch_shapes=[pltpu.VMEM((tm, sw), BF16)],
        compiler_params=pltpu.CompilerParams(
            dimension_semantics=("arbitrary",), vmem_limit_bytes=VMEM_LIMIT),
        name="merge",
    )(h, ya, ys, ua, vn, gs, sgu_w, sgu_b.T, p_attn, p_ssm, p_sgu, w_out, post_g.reshape(1, d))


def kernel(x, ffn1_pre_g, ffn1_w_gu, ffn1_w_down, ffn1_post_g, mix_pre_g, w_in, conv_w, conv_b, dt_bias, a_log, d_skip, ssm_norm_g, sgu_ln_g, sgu_ln_b, sgu_w, sgu_b, p_attn, p_ssm, p_sgu, w_out, mix_post_g, ffn2_pre_g, ffn2_w_gu, ffn2_w_down, ffn2_post_g):
    b, s, d = x.shape
    depth = w_in.shape[0]
    h = x.reshape(b * s, d)
    as_b = lambda a: a.astype(BF16)
    ffn1_w_gu, ffn1_w_down, ffn2_w_gu, ffn2_w_down = map(
        as_b, (ffn1_w_gu, ffn1_w_down, ffn2_w_gu, ffn2_w_down))
    p_attn, p_ssm, p_sgu, w_out = map(as_b, (p_attn, p_ssm, p_sgu, w_out))
    w_pieces = _split_w_in(w_in)
    for i in range(depth):
        h = _ffn(h, ffn1_pre_g[i], ffn1_w_gu, ffn1_w_down, ffn1_post_g[i], i)
        q, k, v, zs, xc, dt_raw, ua, vn, gs = _inproj(
            h, mix_pre_g[i], w_pieces, conv_w[i], conv_b[i], sgu_ln_g[i], sgu_ln_b[i], s, i)
        seq = lambda a: a.reshape(b, s, a.shape[-1])
        ya = _attention(seq(q), seq(k), seq(v))
        ys = _ssm(seq(xc), seq(dt_raw), seq(zs), dt_bias[i], a_log[i], d_skip[i], ssm_norm_g[i])
        h = _merge(h, ya.reshape(b * s, -1), ys.reshape(b * s, -1), ua, vn, gs,
                   sgu_w[i], sgu_b[i], p_attn, p_ssm, p_sgu, w_out, mix_post_g[i], i)
        h = _ffn(h, ffn2_pre_g[i], ffn2_w_gu, ffn2_w_down, ffn2_post_g[i], i)
    return h.reshape(b, s, d)
```

```python
import functools

import jax
import jax.numpy as jnp
from jax import lax
from jax.experimental import pallas as pl
from jax.experimental.pallas import tpu as pltpu

F32 = jnp.float32
BF16 = jnp.bfloat16

NORM_EPS = 1e-6
ATTN_HEADS = 8
ATTN_HEAD_DIM = 128
MOBA_BLOCK = 256
MOBA_TOPK = 3
SSM_HEADS = 16
SSM_HEAD_DIM = 64
SSM_WIDTH = SSM_HEADS * SSM_HEAD_DIM
SSM_GROUPS = 2
SSM_STATE = 128
SSM_CONV = 4
SSM_CHUNK = 128
SSM_BC = SSM_GROUPS * SSM_STATE
SSM_CONV_CH = SSM_WIDTH + 2 * SSM_BC
SGU_GROUPS = 8
SGU_GROUP_DIM = 128
SGU_CHUNK = 128
N_BRANCHES = 3

LANES = 128
SUBLANES = 8
BF16_ROWS = 16
LOG2E = 1.4426950408889634
VMEM_LIMIT = 56 * 1024 * 1024
NEG_BIG = -1e30
POS_BIG = 1e30

FFN_TOKENS = 512
FFN_HIDDEN_CHUNK = 256
INPROJ_TOKENS = 256
INPROJ_COL_CHUNK = 256
SSM_TOKENS = 512
MERGE_TOKENS = 512
ATTN_GROUP = 8


def _resident(shape):
    nd = len(shape)
    return pl.BlockSpec(shape, lambda *_: (0,) * nd, pipeline_mode=pl.Buffered(1))


def _resident_layer(stacked_shape, layer):
    shape = tuple(stacked_shape[1:])
    nd = len(shape)
    return pl.BlockSpec((None,) + shape, lambda *_: (layer,) + (0,) * nd,
                        pipeline_mode=pl.Buffered(1))


def _rms(x, g):
    return x * lax.rsqrt(jnp.mean(x * x, axis=-1, keepdims=True) + NORM_EPS) * g


def _sigmoid(x):
    return 1.0 / (1.0 + jnp.exp(-x))


def _gelu_tanh(x):
    return x * (0.5 * (1.0 + jnp.tanh(0.7978845608028654 * (x + 0.044715 * (x * x * x)))))


def _dot(a, b):
    return jnp.dot(a, b, preferred_element_type=F32)


def _dot_nt(a, b):
    return lax.dot_general(a, b, (((1,), (1,)), ((), ())), preferred_element_type=F32)


def _dot_tn(a, b):
    return lax.dot_general(a, b, (((0,), (0,)), ((), ())), preferred_element_type=F32)


def _split2(x):
    hi = x.astype(BF16)
    lo = (x - hi.astype(F32)).astype(BF16)
    return hi, lo


def _split3(x):
    hi = x.astype(BF16)
    r = x - hi.astype(F32)
    mid = r.astype(BF16)
    lo = (r - mid.astype(F32)).astype(BF16)
    return hi, mid, lo


def _ffn_kernel(h_ref, pre_ref, wgu_ref, wd_ref, post_ref, o_ref, acc_ref):
    x = h_ref[...]
    xn = _rms(x, pre_ref[...]).astype(BF16)
    hidden = wd_ref.shape[0]
    fc = FFN_HIDDEN_CHUNK
    for c in range(hidden // fc):
        g = _dot(xn, wgu_ref[:, c * fc:(c + 1) * fc])
        u = _dot(xn, wgu_ref[:, hidden + c * fc:hidden + (c + 1) * fc])
        a = (g * _sigmoid(g) * u).astype(BF16)
        d = _dot(a, wd_ref[c * fc:(c + 1) * fc, :])
        if c == 0:
            acc_ref[...] = d
        else:
            acc_ref[...] += d
    o_ref[...] = x + 0.5 * _rms(acc_ref[...], post_ref[...])


def _ffn(h, pre_g, w_gu, w_down, post_g, layer):
    t, d = h.shape
    hidden = w_down.shape[1]
    assert hidden % FFN_HIDDEN_CHUNK == 0 and t % FFN_TOKENS == 0
    tm = FFN_TOKENS
    return pl.pallas_call(
        _ffn_kernel,
        out_shape=jax.ShapeDtypeStruct((t, d), F32),
        grid=(t // tm,),
        in_specs=[
            pl.BlockSpec((tm, d), lambda i: (i, 0)),
            _resident((1, d)),
            _resident_layer(w_gu.shape, layer),
            _resident_layer(w_down.shape, layer),
            _resident((1, d)),
        ],
        out_specs=pl.BlockSpec((tm, d), lambda i: (i, 0)),
        scratch_shapes=[pltpu.VMEM((tm, d), F32)],
        compiler_params=pltpu.CompilerParams(
            dimension_semantics=("arbitrary",), vmem_limit_bytes=VMEM_LIMIT),
        name="ffn",
    )(h, pre_g.reshape(1, d), w_gu, w_down, post_g.reshape(1, d))


def _inproj_layout(d_model):
    aw = ATTN_HEADS * ATTN_HEAD_DIM
    sw = SGU_GROUPS * SGU_GROUP_DIM
    return (("q", aw, aw), ("k", aw, aw), ("v", aw, aw), ("z", SSM_WIDTH, SSM_WIDTH),
            ("xbc", SSM_CONV_CH, SSM_CONV_CH), ("dt", SSM_HEADS, LANES),
            ("u", sw, sw), ("vg", sw, sw), ("gates", N_BRANCHES * d_model, N_BRANCHES * d_model))


def _inproj_kernel(*refs, tiles_per_seq, n_pieces):
    h_ref, pre_ref = refs[:2]
    w_refs = refs[2:2 + n_pieces]
    (convw_ref, convb_ref, lng_ref, lnb_ref,
     q_ref, k_ref, v_ref, zs_ref, xc_ref, dt_ref, ua_ref, vn_ref, gs_ref,
     xraw_ref, va_ref, xn_ref) = refs[2 + n_pieces:]
    tm = h_ref.shape[0]
    pad = SUBLANES

    @pl.when(pl.program_id(0) % tiles_per_seq == 0)
    def _():
        xraw_ref[0:pad, :] = jnp.zeros((pad, SSM_CONV_CH), F32)

    xn_ref[...] = _rms(h_ref[...], pre_ref[...]).astype(BF16)

    def store_q(cols, r):
        q_ref[:, cols] = (r * (ATTN_HEAD_DIM ** -0.5 * LOG2E)).astype(BF16)

    def store_plain(o_ref):
        def store(cols, r):
            o_ref[:, cols] = r.astype(o_ref.dtype)
        return store

    def store_silu(cols, r):
        zs_ref[:, cols] = (r * _sigmoid(r)).astype(BF16)

    def store_conv(cols, r):
        xraw_ref[pad:pad + tm, cols] = r
        conv = convb_ref[:, cols] + convw_ref[SSM_CONV - 1:SSM_CONV, cols] * r
        for kk in range(SSM_CONV - 1):
            lag = SSM_CONV - 1 - kk
            conv = conv + convw_ref[kk:kk + 1, cols] * xraw_ref[pad - lag:pad - lag + tm, cols]
        xc_ref[:, cols] = (conv * _sigmoid(conv)).astype(BF16)

    def store_gelu(cols, r):
        ua_ref[:, cols] = _gelu_tanh(r).astype(BF16)

    def store_gelu_f32(cols, r):
        va_ref[:, cols] = _gelu_tanh(r)

    def store_sigmoid(cols, r):
        gs_ref[:, cols] = _sigmoid(r).astype(BF16)

    def layer_norm_v():
        va = va_ref[...]
        mu = jnp.mean(va, axis=-1, keepdims=True)
        var = jnp.mean(jnp.square(va - mu), axis=-1, keepdims=True)
        vn = (va - mu) * lax.rsqrt(var + NORM_EPS) * lng_ref[...] + lnb_ref[...]
        vn_ref[...] = vn.astype(BF16)

    def carry_rows():
        xraw_ref[0:pad, :] = xraw_ref[tm:tm + pad, :]

    stores = dict(q=store_q, k=store_plain(k_ref), v=store_plain(v_ref), z=store_silu,
                  xbc=store_conv, dt=store_plain(dt_ref), u=store_gelu, vg=store_gelu_f32,
                  gates=store_sigmoid)
    after = dict(vg=layer_norm_v, xbc=carry_rows)
    offsets, piece, off = {}, 0, 0
    for name, _, width in _inproj_layout(h_ref.shape[1]):
        if off == w_refs[piece].shape[1]:
            piece, off = piece + 1, 0
        offsets[name] = (w_refs[piece], off, width)
        off += width

    def chunks(names):
        out = []
        for name in names:
            w_ref, off, width = offsets[name]
            starts = list(range(0, width, INPROJ_COL_CHUNK))
            for c0 in starts:
                n = min(INPROJ_COL_CHUNK, width - c0)
                out.append((name, w_ref, off + c0, c0, n, c0 == starts[-1]))
        return out

    heavy = chunks(("vg", "xbc", "u", "z"))
    light = chunks(("q", "k", "v", "dt", "gates"))
    per_heavy = len(light) // len(heavy)
    order = []
    while heavy or light:
        order += heavy[:1] + light[:per_heavy]
        heavy, light = heavy[1:], light[per_heavy:]
    for name, w_ref, w0, c0, n, last in order:
        stores[name](slice(c0, c0 + n), _dot(xn_ref[...], w_ref[:, w0:w0 + n]))
        if last and name in after:
            after[name]()


def _inproj(h, pre_g, w_pieces, conv_w, conv_b, ln_g, ln_b, seq_len, layer):
    t, d = h.shape
    tm = INPROJ_TOKENS
    layout = _inproj_layout(d)
    sw = SGU_GROUPS * SGU_GROUP_DIM
    assert seq_len % tm == 0 and t % seq_len == 0
    assert sum(w.shape[2] for w in w_pieces) == sum(p for _, _, p in layout)
    out_shape = tuple(jax.ShapeDtypeStruct((t, p), F32 if name == "dt" else BF16)
                      for name, _, p in layout)
    out_specs = tuple(pl.BlockSpec((tm, p), lambda i: (i, 0)) for _, _, p in layout)
    return pl.pallas_call(
        functools.partial(_inproj_kernel, tiles_per_seq=seq_len // tm, n_pieces=len(w_pieces)),
        out_shape=out_shape,
        grid=(t // tm,),
        in_specs=[
            pl.BlockSpec((tm, d), lambda i: (i, 0)),
            _resident((1, d)),
            *[_resident_layer(w.shape, layer) for w in w_pieces],
            _resident((SSM_CONV, SSM_CONV_CH)), _resident((1, SSM_CONV_CH)),
            _resident((1, sw)), _resident((1, sw)),
        ],
        out_specs=out_specs,
        scratch_shapes=[
            pltpu.VMEM((tm + SUBLANES, SSM_CONV_CH), F32),
            pltpu.VMEM((tm, sw), F32),
            pltpu.VMEM((tm, d), BF16),
        ],
        compiler_params=pltpu.CompilerParams(
            dimension_semantics=("arbitrary",), vmem_limit_bytes=VMEM_LIMIT),
        name="inproj",
    )(h, pre_g.reshape(1, d), *w_pieces, conv_w, conv_b.reshape(1, -1), ln_g.reshape(1, -1),
      ln_b.reshape(1, -1))


def _inproj_pieces(d_model):
    runs, start, off = [], 0, 0
    for _, width, padded in _inproj_layout(d_model):
        if padded > width:
            if off > start:
                runs.append((start, off, off - start))
            runs.append((off, off + width, padded))
            start = off + width
        off += width
    if off > start:
        runs.append((start, off, off - start))
    return runs


def _split_w_in(w_in):
    depth, d, cols = w_in.shape
    out = []
    for start, stop, padded in _inproj_pieces(d):
        piece = w_in[:, :, start:stop].astype(BF16)
        if padded > stop - start:
            piece = jnp.pad(piece, ((0, 0), (0, 0), (0, padded - (stop - start))))
        out.append(piece)
    assert stop == cols
    return tuple(out)


def _attn_kernel(slope_ref, q_ref, k_ref, v_ref, o_ref, vt_ref, kmean_ref, sel_ref, bias_ref,
                 s_ref, cmax_ref, p_ref, m_ref, acc_ref):
    blk = MOBA_BLOCK
    dh = ATTN_HEAD_DIM
    seq = q_ref.shape[1]
    nb = seq // blk
    gk = bias_ref.shape[0]
    group = gk // blk
    slope = slope_ref[0, 0:1, :] * LOG2E
    slope_row = jnp.concatenate([slope] * (blk // LANES), axis=1)
    bias_ref[...] = lax.broadcasted_iota(jnp.int32, (gk, blk), 0).astype(F32) * slope_row
    m_ref[...] = jnp.full(m_ref.shape, NEG_BIG, F32)
    acc_ref[...] = jnp.zeros_like(acc_ref)
    extra = lax.broadcasted_iota(jnp.int32, (vt_ref.shape[0] - dh, seq), 0)
    vt_ref[dh:, :] = jnp.where(extra == 0, 1.0, 0.0).astype(BF16)

    def prologue(jb, carry):
        off = pl.multiple_of(jb * blk, blk)
        kb = k_ref[0, pl.ds(off, blk), :].astype(F32)
        kmean_ref[pl.ds(jb, 1), :] = jnp.sum(kb, axis=0, keepdims=True) * (1.0 / blk)
        vb = v_ref[0, pl.ds(off, blk), :].astype(F32)
        vt_ref[0:dh, pl.ds(off, blk)] = vb.T.astype(BF16)
        return carry

    lax.fori_loop(0, nb, prologue, 0, unroll=min(8, nb))

    km_parts = jnp.concatenate(_split2(kmean_ref[...]), axis=0)
    gate = _dot_nt(km_parts, q_ref[0])
    gate = gate[0:nb, :] + gate[nb:2 * nb, :]
    blk_ids = lax.broadcasted_iota(jnp.int32, (nb, seq), 0)
    own_blk = lax.shift_right_logical(lax.broadcasted_iota(jnp.int32, (nb, seq), 1),
                                      blk.bit_length() - 1)
    gate = jnp.where(blk_ids < own_blk, gate, -jnp.inf)
    sel = jnp.where(blk_ids == own_blk, 1.0, 0.0)
    blk_ids = blk_ids.astype(F32)
    for _ in range(MOBA_TOPK):
        mx = jnp.max(gate, axis=0, keepdims=True)
        first = jnp.min(jnp.where(gate == mx, blk_ids, float(nb)), axis=0, keepdims=True)
        hit = blk_ids == first
        sel = jnp.where(jnp.logical_and(hit, mx > -jnp.inf), 1.0, sel)
        gate = jnp.where(hit, -jnp.inf, gate)
    sel_ref[...] = sel

    key_minus_qry = (lax.broadcasted_iota(jnp.int32, (blk, blk), 0)
                     - lax.broadcasted_iota(jnp.int32, (blk, blk), 1))

    def score_tile(g, step, slot, nblocks=group, own_block=None):
        r0 = pl.multiple_of(g * gk, gk)
        i = jnp.minimum(step, nb - 1)
        qi = q_ref[0, pl.ds(pl.multiple_of(i * blk, blk), blk), :]
        cms = []
        for u in range(nblocks):
            rows = slice(u * blk, (u + 1) * blk)
            s = _dot_nt(k_ref[0, pl.ds(r0 + u * blk, blk), :], qi) + bias_ref[rows, :]
            if u == own_block:
                s = jnp.where(key_minus_qry <= 0, s, NEG_BIG)
            s_ref[slot, rows, :] = s
            cms.append(jnp.max(s, axis=0, keepdims=True))
        cms += [jnp.full((1, blk), NEG_BIG, F32)] * (group - nblocks)
        cmax_ref[slot] = jnp.concatenate(cms, axis=0)

    def softmax_tile(g, i, slot, nblocks=group):
        r0 = pl.multiple_of(g * gk, gk)
        off = slope_row * jnp.asarray(g * gk - i * blk, F32)
        chosen = sel_ref[pl.ds(pl.multiple_of(g * group, group), group),
                         pl.ds(pl.multiple_of(i * blk, blk), blk)] > 0.0
        m_old = m_ref[pl.ds(i, 1), :]
        m_new = jnp.maximum(m_old, jnp.max(jnp.where(chosen, cmax_ref[slot] + off, NEG_BIG),
                                           axis=0, keepdims=True))
        base = m_new - off
        for u in range(nblocks):
            shift = jnp.where(chosen[u:u + 1, :], base, POS_BIG)
            x = (s_ref[slot, u * blk:(u + 1) * blk, :] - shift).astype(BF16)
            p_ref[u * blk:(u + 1) * blk, :] = jnp.exp2(x)
        pv = _dot(vt_ref[:, pl.ds(r0, nblocks * blk)], p_ref[0:nblocks * blk, :])
        acc_ref[i] = jnp.exp2(m_old - m_new) * acc_ref[i] + pv
        m_ref[pl.ds(i, 1), :] = m_new

    def key_group(g, carry):
        first = g * group
        score_tile(g, first, 0, 1, 0)
        for r in range(group):
            if r + 1 < group:
                score_tile(g, first + r + 1, (r + 1) % 2, r + 2, r + 1)
            else:
                score_tile(g, first + group, group % 2)
            softmax_tile(g, first + r, r % 2, r + 1)

        @pl.when(first + group < nb)
        def _():
            score_tile(g, first + group + 1, 1)

            def tile_pair(i, score_ahead=True):
                for slot in range(2):
                    softmax_tile(g, i + slot, slot)
                    if score_ahead:
                        score_tile(g, i + 2 + slot, slot)

            def later_tiles(t, c):
                tile_pair(first + 2 * t)
                return c

            lax.fori_loop(group // 2, (nb - first) // 2 - 1, later_tiles, 0)
            tile_pair(nb - 2, score_ahead=False)

        return carry

    lax.fori_loop(0, nb // group, key_group, 0)

    def finalize(i, carry):
        a = acc_ref[i]
        out = a[0:dh, :] / a[dh:dh + 1, :]
        o_ref[0, pl.ds(pl.multiple_of(i * blk, blk), blk), :] = out.T.astype(o_ref.dtype)
        return carry

    lax.fori_loop(0, nb, finalize, 0, unroll=min(8, nb))


def _attention(q, k, v):
    b, s, w = q.shape
    dh = ATTN_HEAD_DIM
    nh = w // dh
    assert s % MOBA_BLOCK == 0 and s // MOBA_BLOCK >= MOBA_TOPK and dh == LANES
    nb = s // MOBA_BLOCK
    group = min(ATTN_GROUP, nb)
    assert nb % group == 0 and group % 2 == 0
    gk = group * MOBA_BLOCK
    slopes = 2.0 **(-8.0 * (jnp.arange(nh, dtype=F32) + 1.0) / nh)
    slopes = jnp.broadcast_to(slopes[:, None, None], (nh, SUBLANES, LANES))
    head_spec = pl.BlockSpec((1, s, dh), lambda bi, hi: (bi, 0, hi))
    return pl.pallas_call(
        _attn_kernel,
        out_shape=jax.ShapeDtypeStruct((b, s, w), BF16),
        grid=(b, nh),
        in_specs=[pl.BlockSpec((1, SUBLANES, LANES), lambda bi, hi: (hi, 0, 0)),
                  head_spec, head_spec, head_spec],
        out_specs=head_spec,
        scratch_shapes=[
            pltpu.VMEM((dh + BF16_ROWS, s), BF16),
            pltpu.VMEM((nb, dh), F32),
            pltpu.VMEM((nb, s), F32),
            pltpu.VMEM((gk, MOBA_BLOCK), F32),
            pltpu.VMEM((2, gk, MOBA_BLOCK), F32),
            pltpu.VMEM((2, group, MOBA_BLOCK), F32),
            pltpu.VMEM((gk, MOBA_BLOCK), BF16),
            pltpu.VMEM((nb, MOBA_BLOCK), F32),
            pltpu.VMEM((nb, dh + BF16_ROWS, MOBA_BLOCK), F32),
        ],
        compiler_params=pltpu.CompilerParams(
            dimension_semantics=("arbitrary", "arbitrary"), vmem_limit_bytes=VMEM_LIMIT),
        name="moba_attention",
    )(slopes, q, k, v)


def _ssm_kernel(xc_ref, dt_ref, zs_ref, dtb_ref, alog_ref, dskip_ref,
                normg_ref, expand_ref, o_ref, state_ref, y_ref):
    ts = xc_ref.shape[1]
    q = SSM_CHUNK
    hp = SSM_HEAD_DIM
    gw = SSM_WIDTH // SSM_GROUPS
    hpg = SSM_HEADS // SSM_GROUPS

    @pl.when(pl.program_id(1) == 0)
    def _():
        state_ref[...] = jnp.zeros_like(state_ref)

    a_neg = -jnp.exp(alog_ref[...])
    expand = expand_ref[...]
    row = lax.broadcasted_iota(jnp.int32, (q, q), 0)
    col = lax.broadcasted_iota(jnp.int32, (q, q), 1)
    lower = row >= col
    tri = jnp.where(lower, 1.0, 0.0).astype(BF16)

    def expand_heads(w):
        hi, lo = _split2(w)
        return _dot(hi, expand) + _dot(lo, expand)

    def chunk(c, carry):
        r0 = pl.multiple_of(c * q, q)
        rows = pl.ds(r0, q)
        xs_b = xc_ref[0, rows, 0:SSM_WIDTH]
        xs = xs_b.astype(F32)
        dtr = dt_ref[0, rows, :] + dtb_ref[...]
        dt = jnp.maximum(dtr, 0.0) + jnp.log1p(jnp.exp(-jnp.abs(dtr)))
        da = dt * a_neg
        d_hi, d_mid, d_lo = _split3(da)
        a_cs = _dot(tri, d_hi) + _dot(tri, d_mid) + _dot(tri, d_lo)
        a_cs_t = a_cs.T
        dt_t = dt.T
        a_last = a_cs[q - 1:q, :]

        ea_x = expand_heads(jnp.exp(a_cs))
        wend_x = expand_heads(jnp.exp(a_last - a_cs) * dt)
        cdec_x = ea_x[q - 1:q, :]
        xs_w = (xs * wend_x).astype(BF16)

        for g in range(SSM_GROUPS):
            bg = xc_ref[0, rows, SSM_WIDTH + g * SSM_STATE:SSM_WIDTH + (g + 1) * SSM_STATE]
            cg = xc_ref[0, rows, SSM_WIDTH + SSM_BC + g * SSM_STATE:
                        SSM_WIDTH + SSM_BC + (g + 1) * SSM_STATE]
            cb = _dot_nt(cg, bg)
            lanes = slice(g * gw, (g + 1) * gw)
            st = state_ref[:, lanes]
            y_ref[:, lanes] = _dot(cg, st.astype(BF16)) * ea_x[:, lanes]
            state_ref[:, lanes] = st * cdec_x[:, lanes] + _dot_tn(bg, xs_w[:, lanes])
            for hh in range(0, hpg, 2):
                pair = []
                for h in (g * hpg + hh, g * hpg + hh + 1):
                    seg = a_cs[:, h:h + 1] - a_cs_t[h:h + 1, :]
                    wts = cb * jnp.where(lower, jnp.exp(seg), 0.0) * dt_t[h:h + 1, :]
                    pair.append(_dot(wts.astype(BF16), xs_b[:, h * hp:(h + 1) * hp]))
                h0 = (g * hpg + hh) * hp
                y_ref[:, h0:h0 + 2 * hp] += jnp.concatenate(pair, axis=1)

        y = (y_ref[...] + dskip_ref[...] * xs) * zs_ref[0, rows, :].astype(F32)
        o_ref[0, rows, :] = _rms(y, normg_ref[...]).astype(o_ref.dtype)
        return carry

    lax.fori_loop(0, ts // q, chunk, 0, unroll=4)


def _ssm(xc, dt_raw, zs, dt_bias, a_log, d_skip, norm_g):
    b, s, _ = xc.shape
    ts = min(SSM_TOKENS, s)
    assert s % ts == 0 and ts % SSM_CHUNK == 0 and SSM_HEADS <= LANES

    def lane_pad(x):
        return jnp.zeros((1, LANES), F32).at[0, :SSM_HEADS].set(x)

    head_of_lane = jnp.arange(SSM_WIDTH) // SSM_HEAD_DIM
    expand = (jnp.arange(LANES)[:, None] == head_of_lane[None, :]).astype(BF16)
    tile = lambda w: pl.BlockSpec((1, ts, w), lambda bi, ti: (bi, ti, 0))
    return pl.pallas_call(
        _ssm_kernel,
        out_shape=jax.ShapeDtypeStruct((b, s, SSM_WIDTH), BF16),
        grid=(b, s // ts),
        in_specs=[
            tile(SSM_CONV_CH), tile(LANES), tile(SSM_WIDTH),
            _resident((1, LANES)), _resident((1, LANES)),
            _resident((1, SSM_WIDTH)), _resident((1, SSM_WIDTH)),
            _resident((LANES, SSM_WIDTH)),
        ],
        out_specs=tile(SSM_WIDTH),
        scratch_shapes=[
            pltpu.VMEM((SSM_STATE, SSM_WIDTH), F32),
            pltpu.VMEM((SSM_CHUNK, SSM_WIDTH), F32),
        ],
        compiler_params=pltpu.CompilerParams(
            dimension_semantics=("arbitrary", "arbitrary"), vmem_limit_bytes=VMEM_LIMIT),
        name="ssd",
    )(xc, dt_raw, zs, lane_pad(dt_bias), lane_pad(a_log),
      jnp.repeat(d_skip, SSM_HEAD_DIM).reshape(1, -1), norm_g.reshape(1, -1), expand)


def _merge_kernel(h_ref, ya_ref, ys_ref, ua_ref, vn_ref, gs_ref, sw_ref,
                  sbt_ref, pa_ref, ps_ref, pc_ref, wo_ref, post_ref, o_ref, yc_ref):
    tm, d = h_ref.shape
    ch, gd = SGU_CHUNK, SGU_GROUP_DIM

    row = lax.broadcasted_iota(jnp.int32, (ch, ch), 0)
    col = lax.broadcasted_iota(jnp.int32, (ch, ch), 1)
    for g in range(SGU_GROUPS):
        w_sp = jnp.where(row >= col, sw_ref[g], 0.0).astype(BF16)
        bias = sbt_ref[:, g:g + 1]
        for c in range(tm // ch):
            rs, cs = slice(c * ch, (c + 1) * ch), slice(g * gd, (g + 1) * gd)
            sv = _dot(w_sp, vn_ref[rs, cs]) + bias
            yc_ref[rs, cs] = (ua_ref[rs, cs].astype(F32) * sv).astype(BF16)

    merged = gs_ref[:, 0:d].astype(F32) * _dot(ya_ref[...], pa_ref[...])
    merged += gs_ref[:, d:2 * d].astype(F32) * _dot(ys_ref[...], ps_ref[...])
    merged += gs_ref[:, 2 * d:3 * d].astype(F32) * _dot(yc_ref[...], pc_ref[...])
    m = _dot(merged.astype(BF16), wo_ref[...])
    o_ref[...] = h_ref[...] + _rms(m, post_ref[...])


def _merge(h, ya, ys, ua, vn, gs, sgu_w, sgu_b, p_attn, p_ssm, p_sgu, w_out, post_g, layer):
    t, d = h.shape
    tm = MERGE_TOKENS
    sw = SGU_GROUPS * SGU_GROUP_DIM
    assert t % tm == 0 and tm % SGU_CHUNK == 0
    tile = lambda w: pl.BlockSpec((tm, w), lambda i: (i, 0))
    return pl.pallas_call(
        _merge_kernel,
        out_shape=jax.ShapeDtypeStruct((t, d), F32),
        grid=(t // tm,),
        in_specs=[
            tile(d), tile(ya.shape[1]), tile(ys.shape[1]), tile(sw), tile(sw), tile(N_BRANCHES * d),
            _resident(sgu_w.shape), _resident((SGU_CHUNK, SGU_GROUPS)),
            _resident_layer(p_attn.shape, layer), _resident_layer(p_ssm.shape, layer),
            _resident_layer(p_sgu.shape, layer), _resident_layer(w_out.shape, layer),
            _resident((1, d)),
        ],
        out_specs=tile(d),
        scratch_shapes=[pltpu.VMEM((tm, sw), BF16)],
        compiler_params=pltpu.CompilerParams(
            dimension_semantics=("arbitrary",), vmem_limit_bytes=VMEM_LIMIT),
        name="merge",
    )(h, ya, ys, ua, vn, gs, sgu_w, sgu_b.T, p_attn, p_ssm, p_sgu, w_out, post_g.reshape(1, d))


def kernel(x, ffn1_pre_g, ffn1_w_gu, ffn1_w_down, ffn1_post_g, mix_pre_g, w_in, conv_w, conv_b, dt_bias, a_log, d_skip, ssm_norm_g, sgu_ln_g, sgu_ln_b, sgu_w, sgu_b, p_attn, p_ssm, p_sgu, w_out, mix_post_g, ffn2_pre_g, ffn2_w_gu, ffn2_w_down, ffn2_post_g):
    b, s, d = x.shape
    depth = w_in.shape[0]
    h = x.reshape(b * s, d)
    as_b = lambda a: a.astype(BF16)
    ffn1_w_gu, ffn1_w_down, ffn2_w_gu, ffn2_w_down = map(
        as_b, (ffn1_w_gu, ffn1_w_down, ffn2_w_gu, ffn2_w_down))
    p_attn, p_ssm, p_sgu, w_out = map(as_b, (p_attn, p_ssm, p_sgu, w_out))
    w_pieces = _split_w_in(w_in)
    for i in range(depth):
        h = _ffn(h, ffn1_pre_g[i], ffn1_w_gu, ffn1_w_down, ffn1_post_g[i], i)
        q, k, v, zs, xc, dt_raw, ua, vn, gs = _inproj(
            h, mix_pre_g[i], w_pieces, conv_w[i], conv_b[i], sgu_ln_g[i], sgu_ln_b[i], s, i)
        seq = lambda a: a.reshape(b, s, a.shape[-1])
        ya = _attention(seq(q), seq(k), seq(v))
        ys = _ssm(seq(xc), seq(dt_raw), seq(zs), dt_bias[i], a_log[i], d_skip[i], ssm_norm_g[i])
        h = _merge(h, ya.reshape(b * s, -1), ys.reshape(b * s, -1), ua, vn, gs,
                   sgu_w[i], sgu_b[i], p_attn, p_ssm, p_sgu, w_out, mix_post_g[i], i)
        h = _ffn(h, ffn2_pre_g[i], ffn2_w_gu, ffn2_w_down, ffn2_post_g[i], i)
    return h.reshape(b, s, d)
```

```python
import functools

import jax
import jax.numpy as jnp
from jax import lax
from jax.experimental import pallas as pl
from jax.experimental.pallas import tpu as pltpu

F32 = jnp.float32
BF16 = jnp.bfloat16

NORM_EPS = 1e-6
ATTN_HEADS = 8
ATTN_HEAD_DIM = 128
MOBA_BLOCK = 256
MOBA_TOPK = 3
SSM_HEADS = 16
SSM_HEAD_DIM = 64
SSM_WIDTH = SSM_HEADS * SSM_HEAD_DIM
SSM_GROUPS = 2
SSM_STATE = 128
SSM_CONV = 4
SSM_CHUNK = 128
SSM_BC = SSM_GROUPS * SSM_STATE
SSM_CONV_CH = SSM_WIDTH + 2 * SSM_BC
SGU_GROUPS = 8
SGU_GROUP_DIM = 128
SGU_CHUNK = 128
N_BRANCHES = 3

LANES = 128
SUBLANES = 8
BF16_ROWS = 16
LOG2E = 1.4426950408889634
VMEM_LIMIT = 56 * 1024 * 1024
NEG_BIG = -1e30
POS_BIG = 1e30

FFN_TOKENS = 512
FFN_HIDDEN_CHUNK = 256
INPROJ_TOKENS = 256
INPROJ_COL_CHUNK = 256
SSM_TOKENS = 512
MERGE_TOKENS = 512
ATTN_GROUP = 8


def _resident(shape):
    nd = len(shape)
    return pl.BlockSpec(shape, lambda *_: (0,) * nd, pipeline_mode=pl.Buffered(1))


def _resident_layer(stacked_shape, layer):
    shape = tuple(stacked_shape[1:])
    nd = len(shape)
    return pl.BlockSpec((None,) + shape, lambda *_: (layer,) + (0,) * nd,
                        pipeline_mode=pl.Buffered(1))


def _rms(x, g):
    return x * lax.rsqrt(jnp.mean(x * x, axis=-1, keepdims=True) + NORM_EPS) * g


def _sigmoid(x):
    return 1.0 / (1.0 + jnp.exp(-x))


def _gelu_tanh(x):
    return x * (0.5 * (1.0 + jnp.tanh(0.7978845608028654 * (x + 0.044715 * (x * x * x)))))


def _dot(a, b):
    return jnp.dot(a, b, preferred_element_type=F32)


def _dot_nt(a, b):
    return lax.dot_general(a, b, (((1,), (1,)), ((), ())), preferred_element_type=F32)


def _dot_tn(a, b):
    return lax.dot_general(a, b, (((0,), (0,)), ((), ())), preferred_element_type=F32)


def _split2(x):
    hi = x.astype(BF16)
    lo = (x - hi.astype(F32)).astype(BF16)
    return hi, lo


def _split3(x):
    hi = x.astype(BF16)
    r = x - hi.astype(F32)
    mid = r.astype(BF16)
    lo = (r - mid.astype(F32)).astype(BF16)
    return hi, mid, lo


def _ffn_kernel(h_ref, pre_ref, wgu_ref, wd_ref, post_ref, o_ref, acc_ref):
    x = h_ref[...]
    xn = _rms(x, pre_ref[...]).astype(BF16)
    hidden = wd_ref.shape[0]
    fc = FFN_HIDDEN_CHUNK
    for c in range(hidden // fc):
        g = _dot(xn, wgu_ref[:, c * fc:(c + 1) * fc])
        u = _dot(xn, wgu_ref[:, hidden + c * fc:hidden + (c + 1) * fc])
        a = (g * _sigmoid(g) * u).astype(BF16)
        d = _dot(a, wd_ref[c * fc:(c + 1) * fc, :])
        if c == 0:
            acc_ref[...] = d
        else:
            acc_ref[...] += d
    o_ref[...] = x + 0.5 * _rms(acc_ref[...], post_ref[...])


def _ffn(h, pre_g, w_gu, w_down, post_g, layer):
    t, d = h.shape
    hidden = w_down.shape[1]
    assert hidden % FFN_HIDDEN_CHUNK == 0 and t % FFN_TOKENS == 0
    tm = FFN_TOKENS
    return pl.pallas_call(
        _ffn_kernel,
        out_shape=jax.ShapeDtypeStruct((t, d), F32),
        grid=(t // tm,),
        in_specs=[
            pl.BlockSpec((tm, d), lambda i: (i, 0)),
            _resident((1, d)),
            _resident_layer(w_gu.shape, layer),
            _resident_layer(w_down.shape, layer),
            _resident((1, d)),
        ],
        out_specs=pl.BlockSpec((tm, d), lambda i: (i, 0)),
        scratch_shapes=[pltpu.VMEM((tm, d), F32)],
        compiler_params=pltpu.CompilerParams(
            dimension_semantics=("arbitrary",), vmem_limit_bytes=VMEM_LIMIT),
        name="ffn",
    )(h, pre_g.reshape(1, d), w_gu, w_down, post_g.reshape(1, d))


def _inproj_layout(d_model):
    aw = ATTN_HEADS * ATTN_HEAD_DIM
    sw = SGU_GROUPS * SGU_GROUP_DIM
    return (("q", aw, aw), ("k", aw, aw), ("v", aw, aw), ("z", SSM_WIDTH, SSM_WIDTH),
            ("xbc", SSM_CONV_CH, SSM_CONV_CH), ("dt", SSM_HEADS, LANES),
            ("u", sw, sw), ("vg", sw, sw), ("gates", N_BRANCHES * d_model, N_BRANCHES * d_model))


def _inproj_kernel(*refs, tiles_per_seq, n_pieces):
    h_ref, pre_ref = refs[:2]
    w_refs = refs[2:2 + n_pieces]
    (convw_ref, convb_ref, lng_ref, lnb_ref,
     q_ref, k_ref, v_ref, zs_ref, xc_ref, dt_ref, ua_ref, vn_ref, gs_ref,
     xraw_ref, va_ref, xn_ref) = refs[2 + n_pieces:]
    tm = h_ref.shape[0]
    pad = SUBLANES

    @pl.when(pl.program_id(0) % tiles_per_seq == 0)
    def _():
        xraw_ref[0:pad, :] = jnp.zeros((pad, SSM_CONV_CH), F32)

    xn_ref[...] = _rms(h_ref[...], pre_ref[...]).astype(BF16)

    def store_q(cols, r):
        q_ref[:, cols] = (r * (ATTN_HEAD_DIM ** -0.5 * LOG2E)).astype(BF16)

    def store_plain(o_ref):
        def store(cols, r):
            o_ref[:, cols] = r.astype(o_ref.dtype)
        return store

    def store_silu(cols, r):
        zs_ref[:, cols] = (r * _sigmoid(r)).astype(BF16)

    def store_conv(cols, r):
        xraw_ref[pad:pad + tm, cols] = r
        conv = convb_ref[:, cols] + convw_ref[SSM_CONV - 1:SSM_CONV, cols] * r
        for kk in range(SSM_CONV - 1):
            lag = SSM_CONV - 1 - kk
            conv = conv + convw_ref[kk:kk + 1, cols] * xraw_ref[pad - lag:pad - lag + tm, cols]
        xc_ref[:, cols] = (conv * _sigmoid(conv)).astype(BF16)

    def store_gelu(cols, r):
        ua_ref[:, cols] = _gelu_tanh(r).astype(BF16)

    def store_gelu_f32(cols, r):
        va_ref[:, cols] = _gelu_tanh(r)

    def store_sigmoid(cols, r):
        gs_ref[:, cols] = _sigmoid(r).astype(BF16)

    def layer_norm_v():
        va = va_ref[...]
        mu = jnp.mean(va, axis=-1, keepdims=True)
        var = jnp.mean(jnp.square(va - mu), axis=-1, keepdims=True)
        vn = (va - mu) * lax.rsqrt(var + NORM_EPS) * lng_ref[...] + lnb_ref[...]
        vn_ref[...] = vn.astype(BF16)

    def carry_rows():
        xraw_ref[0:pad, :] = xraw_ref[tm:tm + pad, :]

    stores = dict(q=store_q, k=store_plain(k_ref), v=store_plain(v_ref), z=store_silu,
                  xbc=store_conv, dt=store_plain(dt_ref), u=store_gelu, vg=store_gelu_f32,
                  gates=store_sigmoid)
    after = dict(vg=layer_norm_v, xbc=carry_rows)
    offsets, piece, off = {}, 0, 0
    for name, _, width in _inproj_layout(h_ref.shape[1]):
        if off == w_refs[piece].shape[1]:
            piece, off = piece + 1, 0
        offsets[name] = (w_refs[piece], off, width)
        off += width

    def chunks(names):
        out = []
        for name in names:
            w_ref, off, width = offsets[name]
            starts = list(range(0, width, INPROJ_COL_CHUNK))
            for c0 in starts:
                n = min(INPROJ_COL_CHUNK, width - c0)
                out.append((name, w_ref, off + c0, c0, n, c0 == starts[-1]))
        return out

    heavy = chunks(("vg", "xbc", "u", "z"))
    light = chunks(("q", "k", "v", "dt", "gates"))
    per_heavy = len(light) // len(heavy)
    order = []
    while heavy or light:
        order += heavy[:1] + light[:per_heavy]
        heavy, light = heavy[1:], light[per_heavy:]
    for name, w_ref, w0, c0, n, last in order:
        stores[name](slice(c0, c0 + n), _dot(xn_ref[...], w_ref[:, w0:w0 + n]))
        if last and name in after:
            after[name]()


def _inproj(h, pre_g, w_pieces, conv_w, conv_b, ln_g, ln_b, seq_len, layer):
    t, d = h.shape
    tm = INPROJ_TOKENS
    layout = _inproj_layout(d)
    sw = SGU_GROUPS * SGU_GROUP_DIM
    assert seq_len % tm == 0 and t % seq_len == 0
    assert sum(w.shape[2] for w in w_pieces) == sum(p for _, _, p in layout)
    out_shape = tuple(jax.ShapeDtypeStruct((t, p), F32 if name == "dt" else BF16)
                      for name, _, p in layout)
    out_specs = tuple(pl.BlockSpec((tm, p), lambda i: (i, 0)) for _, _, p in layout)
    return pl.pallas_call(
        functools.partial(_inproj_kernel, tiles_per_seq=seq_len // tm, n_pieces=len(w_pieces)),
        out_shape=out_shape,
        grid=(t // tm,),
        in_specs=[
            pl.BlockSpec((tm, d), lambda i: (i, 0)),
            _resident((1, d)),
            *[_resident_layer(w.shape, layer) for w in w_pieces],
            _resident((SSM_CONV, SSM_CONV_CH)), _resident((1, SSM_CONV_CH)),
            _resident((1, sw)), _resident((1, sw)),
        ],
        out_specs=out_specs,
        scratch_shapes=[
            pltpu.VMEM((tm + SUBLANES, SSM_CONV_CH), F32),
            pltpu.VMEM((tm, sw), F32),
            pltpu.VMEM((tm, d), BF16),
        ],
        compiler_params=pltpu.CompilerParams(
            dimension_semantics=("arbitrary",), vmem_limit_bytes=VMEM_LIMIT),
        name="inproj",
    )(h, pre_g.reshape(1, d), *w_pieces, conv_w, conv_b.reshape(1, -1), ln_g.reshape(1, -1),
      ln_b.reshape(1, -1))


def _inproj_pieces(d_model):
    runs, start, off = [], 0, 0
    for _, width, padded in _inproj_layout(d_model):
        if padded > width:
            if off > start:
                runs.append((start, off, off - start))
            runs.append((off, off + width, padded))
            start = off + width
        off += width
    if off > start:
        runs.append((start, off, off - start))
    return runs


def _split_w_in(w_in):
    depth, d, cols = w_in.shape
    out = []
    for start, stop, padded in _inproj_pieces(d):
        piece = w_in[:, :, start:stop].astype(BF16)
        if padded > stop - start:
            piece = jnp.pad(piece, ((0, 0), (0, 0), (0, padded - (stop - start))))
        out.append(piece)
    assert stop == cols
    return tuple(out)


def _attn_kernel(slope_ref, q_ref, k_ref, v_ref, o_ref, vt_ref, kmean_ref, sel_ref, bias_ref,
                 s_ref, cmax_ref, p_ref, m_ref, acc_ref):
    blk = MOBA_BLOCK
    dh = ATTN_HEAD_DIM
    seq = q_ref.shape[1]
    nb = seq // blk
    gk = bias_ref.shape[0]
    group = gk // blk
    slope = slope_ref[0, 0:1, :] * LOG2E
    slope_row = jnp.concatenate([slope] * (blk // LANES), axis=1)
    bias_ref[...] = lax.broadcasted_iota(jnp.int32, (gk, blk), 0).astype(F32) * slope_row
    m_ref[...] = jnp.full(m_ref.shape, NEG_BIG, F32)
    acc_ref[...] = jnp.zeros_like(acc_ref)
    extra = lax.broadcasted_iota(jnp.int32, (vt_ref.shape[0] - dh, seq), 0)
    vt_ref[dh:, :] = jnp.where(extra == 0, 1.0, 0.0).astype(BF16)

    def prologue(jb, carry):
        off = pl.multiple_of(jb * blk, blk)
        kb = k_ref[0, pl.ds(off, blk), :].astype(F32)
        kmean_ref[pl.ds(jb, 1), :] = jnp.sum(kb, axis=0, keepdims=True) * (1.0 / blk)
        vb = v_ref[0, pl.ds(off, blk), :].astype(F32)
        vt_ref[0:dh, pl.ds(off, blk)] = vb.T.astype(BF16)
        return carry

    lax.fori_loop(0, nb, prologue, 0, unroll=min(8, nb))

    km_parts = jnp.concatenate(_split2(kmean_ref[...]), axis=0)
    gate = _dot_nt(km_parts, q_ref[0])
    gate = gate[0:nb, :] + gate[nb:2 * nb, :]
    blk_ids = lax.broadcasted_iota(jnp.int32, (nb, seq), 0)
    own_blk = lax.shift_right_logical(lax.broadcasted_iota(jnp.int32, (nb, seq), 1),
                                      blk.bit_length() - 1)
    gate = jnp.where(blk_ids < own_blk, gate, -jnp.inf)
    sel = jnp.where(blk_ids == own_blk, 1.0, 0.0)
    blk_ids = blk_ids.astype(F32)
    for _ in range(MOBA_TOPK):
        mx = jnp.max(gate, axis=0, keepdims=True)
        first = jnp.min(jnp.where(gate == mx, blk_ids, float(nb)), axis=0, keepdims=True)
        hit = blk_ids == first
        sel = jnp.where(jnp.logical_and(hit, mx > -jnp.inf), 1.0, sel)
        gate = jnp.where(hit, -jnp.inf, gate)
    sel_ref[...] = sel

    key_minus_qry = (lax.broadcasted_iota(jnp.int32, (blk, blk), 0)
                     - lax.broadcasted_iota(jnp.int32, (blk, blk), 1))

    def score_tile(g, step, slot, nblocks=group, own_block=None):
        r0 = pl.multiple_of(g * gk, gk)
        i = jnp.minimum(step, nb - 1)
        qi = q_ref[0, pl.ds(pl.multiple_of(i * blk, blk), blk), :]
        cms = []
        for u in range(nblocks):
            rows = slice(u * blk, (u + 1) * blk)
            s = _dot_nt(k_ref[0, pl.ds(r0 + u * blk, blk), :], qi) + bias_ref[rows, :]
            if u == own_block:
                s = jnp.where(key_minus_qry <= 0, s, NEG_BIG)
            s_ref[slot, rows, :] = s
            cms.append(jnp.max(s, axis=0, keepdims=True))
        cms += [jnp.full((1, blk), NEG_BIG, F32)] * (group - nblocks)
        cmax_ref[slot] = jnp.concatenate(cms, axis=0)

    def softmax_tile(g, i, slot, nblocks=group):
        r0 = pl.multiple_of(g * gk, gk)
        off = slope_row * jnp.asarray(g * gk - i * blk, F32)
        chosen = sel_ref[pl.ds(pl.multiple_of(g * group, group), group),
                         pl.ds(pl.multiple_of(i * blk, blk), blk)] > 0.0
        m_old = m_ref[pl.ds(i, 1), :]
        m_new = jnp.maximum(m_old, jnp.max(jnp.where(chosen, cmax_ref[slot] + off, NEG_BIG),
                                           axis=0, keepdims=True))
        base = m_new - off
        for u in range(nblocks):
            shift = jnp.where(chosen[u:u + 1, :], base, POS_BIG)
            x = (s_ref[slot, u * blk:(u + 1) * blk, :] - shift).astype(BF16)
            p_ref[u * blk:(u + 1) * blk, :] = jnp.exp2(x)
        pv = _dot(vt_ref[:, pl.ds(r0, nblocks * blk)], p_ref[0:nblocks * blk, :])
        acc_ref[i] = jnp.exp2(m_old - m_new) * acc_ref[i] + pv
        m_ref[pl.ds(i, 1), :] = m_new

    def key_group(g, carry):
        first = g * group
        score_tile(g, first, 0, 1, 0)
        for r in range(group):
            if r + 1 < group:
                score_tile(g, first + r + 1, (r + 1) % 2, r + 2, r + 1)
            else:
                score_tile(g, first + group, group % 2)
            softmax_tile(g, first + r, r % 2, r + 1)

        @pl.when(first + group < nb)
        def _():
            score_tile(g, first + group + 1, 1)

            def tile_pair(i, score_ahead=True):
                for slot in range(2):
                    softmax_tile(g, i + slot, slot)
                    if score_ahead:
                        score_tile(g, i + 2 + slot, slot)

            def later_tiles(t, c):
                tile_pair(first + 2 * t)
                return c

            lax.fori_loop(group // 2, (nb - first) // 2 - 1, later_tiles, 0)
            tile_pair(nb - 2, score_ahead=False)

        return carry

    lax.fori_loop(0, nb // group, key_group, 0)

    def finalize(i, carry):
        a = acc_ref[i]
        out = a[0:dh, :] / a[dh:dh + 1, :]
        o_ref[0, pl.ds(pl.multiple_of(i * blk, blk), blk), :] = out.T.astype(o_ref.dtype)
        return carry

    lax.fori_loop(0, nb, finalize, 0, unroll=min(8, nb))


def _attention(q, k, v):
    b, s, w = q.shape
    dh = ATTN_HEAD_DIM
    nh = w // dh
    assert s % MOBA_BLOCK == 0 and s // MOBA_BLOCK >= MOBA_TOPK and dh == LANES
    nb = s // MOBA_BLOCK
    group = min(ATTN_GROUP, nb)
    assert nb % group == 0 and group % 2 == 0
    gk = group * MOBA_BLOCK
    slopes = 2.0 **(-8.0 * (jnp.arange(nh, dtype=F32) + 1.0) / nh)
    slopes = jnp.broadcast_to(slopes[:, None, None], (nh, SUBLANES, LANES))
    head_spec = pl.BlockSpec((1, s, dh), lambda bi, hi: (bi, 0, hi))
    return pl.pallas_call(
        _attn_kernel,
        out_shape=jax.ShapeDtypeStruct((b, s, w), BF16),
        grid=(b, nh),
        in_specs=[pl.BlockSpec((1, SUBLANES, LANES), lambda bi, hi: (hi, 0, 0)),
                  head_spec, head_spec, head_spec],
        out_specs=head_spec,
        scratch_shapes=[
            pltpu.VMEM((dh + BF16_ROWS, s), BF16),
            pltpu.VMEM((nb, dh), F32),
            pltpu.VMEM((nb, s), F32),
            pltpu.VMEM((gk, MOBA_BLOCK), F32),
            pltpu.VMEM((2, gk, MOBA_BLOCK), F32),
            pltpu.VMEM((2, group, MOBA_BLOCK), F32),
            pltpu.VMEM((gk, MOBA_BLOCK), BF16),
            pltpu.VMEM((nb, MOBA_BLOCK), F32),
            pltpu.VMEM((nb, dh + BF16_ROWS, MOBA_BLOCK), F32),
        ],
        compiler_params=pltpu.CompilerParams(
            dimension_semantics=("arbitrary", "arbitrary"), vmem_limit_bytes=VMEM_LIMIT),
        name="moba_attention",
    )(slopes, q, k, v)


def _ssm_kernel(xc_ref, dt_ref, zs_ref, dtb_ref, alog_ref, dskip_ref,
                normg_ref, expand_ref, o_ref, state_ref, y_ref):
    ts = xc_ref.shape[1]
    q = SSM_CHUNK
    hp = SSM_HEAD_DIM
    gw = SSM_WIDTH // SSM_GROUPS
    hpg = SSM_HEADS // SSM_GROUPS

    @pl.when(pl.program_id(1) == 0)
    def _():
        state_ref[...] = jnp.zeros_like(state_ref)

    a_neg = -jnp.exp(alog_ref[...])
    expand = expand_ref[...]
    row = lax.broadcasted_iota(jnp.int32, (q, q), 0)
    col = lax.broadcasted_iota(jnp.int32, (q, q), 1)
    lower = row >= col
    tri = jnp.where(lower, 1.0, 0.0).astype(BF16)
    first_head_lanes = lax.broadcasted_iota(jnp.int32, (q, 2 * hp), 1) < hp

    def expand_heads(w):
        hi, lo = _split2(w)
        return _dot(hi, expand) + _dot(lo, expand)

    def chunk(c, carry):
        r0 = pl.multiple_of(c * q, q)
        rows = pl.ds(r0, q)
        xs_b = xc_ref[0, rows, 0:SSM_WIDTH]
        xs = xs_b.astype(F32)
        dtr = dt_ref[0, rows, :] + dtb_ref[...]
        dt = jnp.maximum(dtr, 0.0) + jnp.log1p(jnp.exp(-jnp.abs(dtr)))
        da = dt * a_neg
        d_hi, d_mid, d_lo = _split3(da)
        a_cs = _dot(tri, d_hi) + _dot(tri, d_mid) + _dot(tri, d_lo)
        a_cs_t = a_cs.T
        dt_t = dt.T
        a_last = a_cs[q - 1:q, :]

        ea_x = expand_heads(jnp.exp(a_cs))
        wend_x = _dot((jnp.exp(a_last - a_cs) * dt).astype(BF16), expand)
        cdec_x = ea_x[q - 1:q, :]
        xs_w = (xs * wend_x).astype(BF16)

        for g in range(SSM_GROUPS):
            bg = xc_ref[0, rows, SSM_WIDTH + g * SSM_STATE:SSM_WIDTH + (g + 1) * SSM_STATE]
            cg = xc_ref[0, rows, SSM_WIDTH + SSM_BC + g * SSM_STATE:
                        SSM_WIDTH + SSM_BC + (g + 1) * SSM_STATE]
            cb = _dot_nt(cg, bg)
            lanes = slice(g * gw, (g + 1) * gw)
            st = state_ref[:, lanes]
            y_ref[:, lanes] = _dot(cg, st.astype(BF16)) * ea_x[:, lanes]
            state_ref[:, lanes] = st * cdec_x[:, lanes] + _dot_tn(bg, xs_w[:, lanes])
            for hh in range(0, hpg, 2):
                wts = []
                for h in (g * hpg + hh, g * hpg + hh + 1):
                    seg = a_cs[:, h:h + 1] - a_cs_t[h:h + 1, :]
                    w = cb * jnp.where(lower, jnp.exp(seg), 0.0) * dt_t[h:h + 1, :]
                    wts.append(w.astype(BF16))
                h0 = (g * hpg + hh) * hp
                x_pair = xs_b[:, h0:h0 + 2 * hp]
                x_diag = jnp.concatenate([jnp.where(first_head_lanes, x_pair, 0),
                                          jnp.where(first_head_lanes, 0, x_pair)], axis=0)
                y_ref[:, h0:h0 + 2 * hp] += _dot(jnp.concatenate(wts, axis=1), x_diag)

        y = (y_ref[...] + dskip_ref[...] * xs) * zs_ref[0, rows, :].astype(F32)
        o_ref[0, rows, :] = _rms(y, normg_ref[...]).astype(o_ref.dtype)
        return carry

    lax.fori_loop(0, ts // q, chunk, 0, unroll=4)


def _ssm(xc, dt_raw, zs, dt_bias, a_log, d_skip, norm_g):
    b, s, _ = xc.shape
    ts = min(SSM_TOKENS, s)
    assert s % ts == 0 and ts % SSM_CHUNK == 0 and SSM_HEADS <= LANES

    def lane_pad(x):
        return jnp.zeros((1, LANES), F32).at[0, :SSM_HEADS].set(x)

    head_of_lane = jnp.arange(SSM_WIDTH) // SSM_HEAD_DIM
    expand = (jnp.arange(LANES)[:, None] == head_of_lane[None, :]).astype(BF16)
    tile = lambda w: pl.BlockSpec((1, ts, w), lambda bi, ti: (bi, ti, 0))
    return pl.pallas_call(
        _ssm_kernel,
        out_shape=jax.ShapeDtypeStruct((b, s, SSM_WIDTH), BF16),
        grid=(b, s // ts),
        in_specs=[
            tile(SSM_CONV_CH), tile(LANES), tile(SSM_WIDTH),
            _resident((1, LANES)), _resident((1, LANES)),
            _resident((1, SSM_WIDTH)), _resident((1, SSM_WIDTH)),
            _resident((LANES, SSM_WIDTH)),
        ],
        out_specs=tile(SSM_WIDTH),
        scratch_shapes=[
            pltpu.VMEM((SSM_STATE, SSM_WIDTH), F32),
            pltpu.VMEM((SSM_CHUNK, SSM_WIDTH), F32),
        ],
        compiler_params=pltpu.CompilerParams(
            dimension_semantics=("arbitrary", "arbitrary"), vmem_limit_bytes=VMEM_LIMIT),
        name="ssd",
    )(xc, dt_raw, zs, lane_pad(dt_bias), lane_pad(a_log),
      jnp.repeat(d_skip, SSM_HEAD_DIM).reshape(1, -1), norm_g.reshape(1, -1), expand)


def _merge_kernel(h_ref, ya_ref, ys_ref, ua_ref, vn_ref, gs_ref, sw_ref,
                  sbt_ref, pa_ref, ps_ref, pc_ref, wo_ref, post_ref, o_ref, yc_ref):
    tm, d = h_ref.shape
    ch, gd = SGU_CHUNK, SGU_GROUP_DIM

    row = lax.broadcasted_iota(jnp.int32, (ch, ch), 0)
    col = lax.broadcasted_iota(jnp.int32, (ch, ch), 1)
    for g in range(SGU_GROUPS):
        w_sp = jnp.where(row >= col, sw_ref[g], 0.0).astype(BF16)
        bias = sbt_ref[:, g:g + 1]
        for c in range(tm // ch):
            rs, cs = slice(c * ch, (c + 1) * ch), slice(g * gd, (g + 1) * gd)
            sv = _dot(w_sp, vn_ref[rs, cs]) + bias
            yc_ref[rs, cs] = (ua_ref[rs, cs].astype(F32) * sv).astype(BF16)

    merged = gs_ref[:, 0:d].astype(F32) * _dot(ya_ref[...], pa_ref[...])
    merged += gs_ref[:, d:2 * d].astype(F32) * _dot(ys_ref[...], ps_ref[...])
    merged += gs_ref[:, 2 * d:3 * d].astype(F32) * _dot(yc_ref[...], pc_ref[...])
    m = _dot(merged.astype(BF16), wo_ref[...])
    o_ref[...] = h_ref[...] + _rms(m, post_ref[...])


def _merge(h, ya, ys, ua, vn, gs, sgu_w, sgu_b, p_attn, p_ssm, p_sgu, w_out, post_g, layer):
    t, d = h.shape
    tm = MERGE_TOKENS
    sw = SGU_GROUPS * SGU_GROUP_DIM
    assert t % tm == 0 and tm % SGU_CHUNK == 0
    tile = lambda w: pl.BlockSpec((tm, w), lambda i: (i, 0))
    return pl.pallas_call(
        _merge_kernel,
        out_shape=jax.ShapeDtypeStruct((t, d), F32),
        grid=(t // tm,),
        in_specs=[
            tile(d), tile(ya.shape[1]), tile(ys.shape[1]), tile(sw), tile(sw), tile(N_BRANCHES * d),
            _resident(sgu_w.shape), _resident((SGU_CHUNK, SGU_GROUPS)),
            _resident_layer(p_attn.shape, layer), _resident_layer(p_ssm.shape, layer),
            _resident_layer(p_sgu.shape, layer), _resident_layer(w_out.shape, layer),
            _resident((1, d)),
        ],
        out_specs=tile(d),
        scratch_shapes=[pltpu.VMEM((tm, sw), BF16)],
        compiler_params=pltpu.CompilerParams(
            dimension_semantics=("arbitrary",), vmem_limit_bytes=VMEM_LIMIT),
        name="merge",
    )(h, ya, ys, ua, vn, gs, sgu_w, sgu_b.T, p_attn, p_ssm, p_sgu, w_out, post_g.reshape(1, d))


def kernel(x, ffn1_pre_g, ffn1_w_gu, ffn1_w_down, ffn1_post_g, mix_pre_g, w_in, conv_w, conv_b, dt_bias, a_log, d_skip, ssm_norm_g, sgu_ln_g, sgu_ln_b, sgu_w, sgu_b, p_attn, p_ssm, p_sgu, w_out, mix_post_g, ffn2_pre_g, ffn2_w_gu, ffn2_w_down, ffn2_post_g):
    b, s, d = x.shape
    depth = w_in.shape[0]
    h = x.reshape(b * s, d)
    as_b = lambda a: a.astype(BF16)
    ffn1_w_gu, ffn1_w_down, ffn2_w_gu, ffn2_w_down = map(
        as_b, (ffn1_w_gu, ffn1_w_down, ffn2_w_gu, ffn2_w_down))
    p_attn, p_ssm, p_sgu, w_out = map(as_b, (p_attn, p_ssm, p_sgu, w_out))
    w_pieces = _split_w_in(w_in)
    for i in range(depth):
        h = _ffn(h, ffn1_pre_g[i], ffn1_w_gu, ffn1_w_down, ffn1_post_g[i], i)
        q, k, v, zs, xc, dt_raw, ua, vn, gs = _inproj(
            h, mix_pre_g[i], w_pieces, conv_w[i], conv_b[i], sgu_ln_g[i], sgu_ln_b[i], s, i)
        seq = lambda a: a.reshape(b, s, a.shape[-1])
        ya = _attention(seq(q), seq(k), seq(v))
        ys = _ssm(seq(xc), seq(dt_raw), seq(zs), dt_bias[i], a_log[i], d_skip[i], ssm_norm_g[i])
        h = _merge(h, ya.reshape(b * s, -1), ys.reshape(b * s, -1), ua, vn, gs,
                   sgu_w[i], sgu_b[i], p_attn, p_ssm, p_sgu, w_out, mix_post_g[i], i)
        h = _ffn(h, ffn2_pre_g[i], ffn2_w_gu, ffn2_w_down, ffn2_post_g[i], i)
    return h.reshape(b, s, d)
```

```python
import functools

import jax
import jax.numpy as jnp
from jax import lax
from jax.experimental import pallas as pl
from jax.experimental.pallas import tpu as pltpu

F32 = jnp.float32
BF16 = jnp.bfloat16

NORM_EPS = 1e-6
ATTN_HEADS = 8
ATTN_HEAD_DIM = 128
MOBA_BLOCK = 256
MOBA_TOPK = 3
SSM_HEADS = 16
SSM_HEAD_DIM = 64
SSM_WIDTH = SSM_HEADS * SSM_HEAD_DIM
SSM_GROUPS = 2
SSM_STATE = 128
SSM_CONV = 4
SSM_CHUNK = 128
SSM_BC = SSM_GROUPS * SSM_STATE
SSM_CONV_CH = SSM_WIDTH + 2 * SSM_BC
SGU_GROUPS = 8
SGU_GROUP_DIM = 128
SGU_CHUNK = 128
N_BRANCHES = 3

LANES = 128
SUBLANES = 8
BF16_ROWS = 16
LOG2E = 1.4426950408889634
VMEM_LIMIT = 56 * 1024 * 1024
NEG_BIG = -1e30
POS_BIG = 1e30

FFN_TOKENS = 512
FFN_HIDDEN_CHUNK = 256
INPROJ_TOKENS = 256
INPROJ_COL_CHUNK = 256
SSM_TOKENS = 512
MERGE_TOKENS = 512
ATTN_GROUP = 8


def _resident(shape):
    nd = len(shape)
    return pl.BlockSpec(shape, lambda *_: (0,) * nd, pipeline_mode=pl.Buffered(1))


def _resident_layer(stacked_shape, layer):
    shape = tuple(stacked_shape[1:])
    nd = len(shape)
    return pl.BlockSpec((None,) + shape, lambda *_: (layer,) + (0,) * nd,
                        pipeline_mode=pl.Buffered(1))


def _rms(x, g):
    return x * lax.rsqrt(jnp.mean(x * x, axis=-1, keepdims=True) + NORM_EPS) * g


def _sigmoid(x):
    return 1.0 / (1.0 + jnp.exp(-x))


def _gelu_tanh(x):
    return x * (0.5 * (1.0 + jnp.tanh(0.7978845608028654 * (x + 0.044715 * (x * x * x)))))


def _dot(a, b):
    return jnp.dot(a, b, preferred_element_type=F32)


def _dot_nt(a, b):
    return lax.dot_general(a, b, (((1,), (1,)), ((), ())), preferred_element_type=F32)


def _dot_tn(a, b):
    return lax.dot_general(a, b, (((0,), (0,)), ((), ())), preferred_element_type=F32)


def _split2(x):
    hi = x.astype(BF16)
    lo = (x - hi.astype(F32)).astype(BF16)
    return hi, lo


def _split3(x):
    hi = x.astype(BF16)
    r = x - hi.astype(F32)
    mid = r.astype(BF16)
    lo = (r - mid.astype(F32)).astype(BF16)
    return hi, mid, lo


def _ffn_kernel(h_ref, *refs):
    o_ref, acc_ref = refs[-2:]
    fc = FFN_HIDDEN_CHUNK
    x = h_ref[...]
    for k in range((len(refs) - 2) // 4):
        pre_ref, wgu_ref, wd_ref, post_ref = refs[4 * k:4 * k + 4]
        xn = _rms(x, pre_ref[...]).astype(BF16)
        hidden = wd_ref.shape[0]
        for c in range(hidden // fc):
            g = _dot(xn, wgu_ref[:, c * fc:(c + 1) * fc])
            u = _dot(xn, wgu_ref[:, hidden + c * fc:hidden + (c + 1) * fc])
            a = (g * _sigmoid(g) * u).astype(BF16)
            d = _dot(a, wd_ref[c * fc:(c + 1) * fc, :])
            if c == 0:
                acc_ref[...] = d
            else:
                acc_ref[...] += d
        x = x + 0.5 * _rms(acc_ref[...], post_ref[...])
    o_ref[...] = x


def _ffn(h, stages):
    t, d = h.shape
    tm = FFN_TOKENS
    assert t % tm == 0
    in_specs, args = [pl.BlockSpec((tm, d), lambda i: (i, 0))], [h]
    for pre_g, w_gu, w_down, post_g, layer in stages:
        assert w_down.shape[1] % FFN_HIDDEN_CHUNK == 0
        in_specs += [_resident((1, d)), _resident_layer(w_gu.shape, layer),
                     _resident_layer(w_down.shape, layer), _resident((1, d))]
        args += [pre_g.reshape(1, d), w_gu, w_down, post_g.reshape(1, d)]
    return pl.pallas_call(
        _ffn_kernel,
        out_shape=jax.ShapeDtypeStruct((t, d), F32),
        grid=(t // tm,),
        in_specs=in_specs,
        out_specs=pl.BlockSpec((tm, d), lambda i: (i, 0)),
        scratch_shapes=[pltpu.VMEM((tm, d), F32)],
        compiler_params=pltpu.CompilerParams(
            dimension_semantics=("arbitrary",), vmem_limit_bytes=VMEM_LIMIT),
        name="ffn",
    )(*args)


def _inproj_layout(d_model):
    aw = ATTN_HEADS * ATTN_HEAD_DIM
    sw = SGU_GROUPS * SGU_GROUP_DIM
    return (("q", aw, aw), ("k", aw, aw), ("v", aw, aw), ("z", SSM_WIDTH, SSM_WIDTH),
            ("xbc", SSM_CONV_CH, SSM_CONV_CH), ("dt", SSM_HEADS, LANES),
            ("u", sw, sw), ("vg", sw, sw), ("gates", N_BRANCHES * d_model, N_BRANCHES * d_model))


def _inproj_kernel(*refs, tiles_per_seq, n_pieces):
    h_ref, pre_ref = refs[:2]
    w_refs = refs[2:2 + n_pieces]
    (convw_ref, convb_ref, lng_ref, lnb_ref,
     q_ref, k_ref, v_ref, zs_ref, xc_ref, dt_ref, ua_ref, vn_ref, gs_ref,
     xraw_ref, va_ref, xn_ref) = refs[2 + n_pieces:]
    tm = h_ref.shape[0]
    pad = SUBLANES

    @pl.when(pl.program_id(0) % tiles_per_seq == 0)
    def _():
        xraw_ref[0:pad, :] = jnp.zeros((pad, SSM_CONV_CH), F32)

    xn_ref[...] = _rms(h_ref[...], pre_ref[...]).astype(BF16)

    def store_q(cols, r):
        q_ref[:, cols] = (r * (ATTN_HEAD_DIM ** -0.5 * LOG2E)).astype(BF16)

    def store_plain(o_ref):
        def store(cols, r):
            o_ref[:, cols] = r.astype(o_ref.dtype)
        return store

    def store_silu(cols, r):
        zs_ref[:, cols] = (r * _sigmoid(r)).astype(BF16)

    def store_conv(cols, r):
        xraw_ref[pad:pad + tm, cols] = r
        conv = convb_ref[:, cols] + convw_ref[SSM_CONV - 1:SSM_CONV, cols] * r
        for kk in range(SSM_CONV - 1):
            lag = SSM_CONV - 1 - kk
            conv = conv + convw_ref[kk:kk + 1, cols] * xraw_ref[pad - lag:pad - lag + tm, cols]
        xc_ref[:, cols] = (conv * _sigmoid(conv)).astype(BF16)

    def store_gelu(cols, r):
        ua_ref[:, cols] = _gelu_tanh(r).astype(BF16)

    def store_gelu_f32(cols, r):
        va_ref[:, cols] = _gelu_tanh(r)

    def store_sigmoid(cols, r):
        gs_ref[:, cols] = _sigmoid(r).astype(BF16)

    def layer_norm_v():
        va = va_ref[...]
        mu = jnp.mean(va, axis=-1, keepdims=True)
        var = jnp.mean(jnp.square(va - mu), axis=-1, keepdims=True)
        vn = (va - mu) * lax.rsqrt(var + NORM_EPS) * lng_ref[...] + lnb_ref[...]
        vn_ref[...] = vn.astype(BF16)

    def carry_rows():
        xraw_ref[0:pad, :] = xraw_ref[tm:tm + pad, :]

    stores = dict(q=store_q, k=store_plain(k_ref), v=store_plain(v_ref), z=store_silu,
                  xbc=store_conv, dt=store_plain(dt_ref), u=store_gelu, vg=store_gelu_f32,
                  gates=store_sigmoid)
    after = dict(vg=layer_norm_v, xbc=carry_rows)
    offsets, piece, off = {}, 0, 0
    for name, _, width in _inproj_layout(h_ref.shape[1]):
        if off == w_refs[piece].shape[1]:
            piece, off = piece + 1, 0
        offsets[name] = (w_refs[piece], off, width)
        off += width

    def chunks(names):
        out = []
        for name in names:
            w_ref, off, width = offsets[name]
            starts = list(range(0, width, INPROJ_COL_CHUNK))
            for c0 in starts:
                n = min(INPROJ_COL_CHUNK, width - c0)
                out.append((name, w_ref, off + c0, c0, n, c0 == starts[-1]))
        return out

    heavy = chunks(("vg", "xbc", "u", "z"))
    light = chunks(("q", "k", "v", "dt", "gates"))
    per_heavy = len(light) // len(heavy)
    order = []
    while heavy or light:
        order += heavy[:1] + light[:per_heavy]
        heavy, light = heavy[1:], light[per_heavy:]
    for name, w_ref, w0, c0, n, last in order:
        stores[name](slice(c0, c0 + n), _dot(xn_ref[...], w_ref[:, w0:w0 + n]))
        if last and name in after:
            after[name]()


def _inproj(h, pre_g, w_pieces, conv_w, conv_b, ln_g, ln_b, seq_len, layer):
    t, d = h.shape
    tm = INPROJ_TOKENS
    layout = _inproj_layout(d)
    sw = SGU_GROUPS * SGU_GROUP_DIM
    assert seq_len % tm == 0 and t % seq_len == 0
    assert sum(w.shape[2] for w in w_pieces) == sum(p for _, _, p in layout)
    out_shape = tuple(jax.ShapeDtypeStruct((t, p), F32 if name == "dt" else BF16)
                      for name, _, p in layout)
    out_specs = tuple(pl.BlockSpec((tm, p), lambda i: (i, 0)) for _, _, p in layout)
    return pl.pallas_call(
        functools.partial(_inproj_kernel, tiles_per_seq=seq_len // tm, n_pieces=len(w_pieces)),
        out_shape=out_shape,
        grid=(t // tm,),
        in_specs=[
            pl.BlockSpec((tm, d), lambda i: (i, 0)),
            _resident((1, d)),
            *[_resident_layer(w.shape, layer) for w in w_pieces],
            _resident((SSM_CONV, SSM_CONV_CH)), _resident((1, SSM_CONV_CH)),
            _resident((1, sw)), _resident((1, sw)),
        ],
        out_specs=out_specs,
        scratch_shapes=[
            pltpu.VMEM((tm + SUBLANES, SSM_CONV_CH), F32),
            pltpu.VMEM((tm, sw), F32),
            pltpu.VMEM((tm, d), BF16),
        ],
        compiler_params=pltpu.CompilerParams(
            dimension_semantics=("arbitrary",), vmem_limit_bytes=VMEM_LIMIT),
        name="inproj",
    )(h, pre_g.reshape(1, d), *w_pieces, conv_w, conv_b.reshape(1, -1), ln_g.reshape(1, -1),
      ln_b.reshape(1, -1))


def _inproj_pieces(d_model):
    runs, start, off = [], 0, 0
    for _, width, padded in _inproj_layout(d_model):
        if padded > width:
            if off > start:
                runs.append((start, off, off - start))
            runs.append((off, off + width, padded))
            start = off + width
        off += width
    if off > start:
        runs.append((start, off, off - start))
    return runs


def _split_w_in(w_in):
    depth, d, cols = w_in.shape
    out = []
    for start, stop, padded in _inproj_pieces(d):
        piece = w_in[:, :, start:stop].astype(BF16)
        if padded > stop - start:
            piece = jnp.pad(piece, ((0, 0), (0, 0), (0, padded - (stop - start))))
        out.append(piece)
    assert stop == cols
    return tuple(out)


def _attn_kernel(slope_ref, q_ref, k_ref, v_ref, o_ref, vt_ref, kmean_ref, sel_ref, bias_ref,
                 s_ref, cmax_ref, p_ref, m_ref, acc_ref):
    blk = MOBA_BLOCK
    dh = ATTN_HEAD_DIM
    seq = q_ref.shape[1]
    nb = seq // blk
    gk = bias_ref.shape[0]
    group = gk // blk
    slope = slope_ref[0, 0:1, :] * LOG2E
    slope_row = jnp.concatenate([slope] * (blk // LANES), axis=1)
    bias_ref[...] = lax.broadcasted_iota(jnp.int32, (gk, blk), 0).astype(F32) * slope_row
    m_ref[...] = jnp.full(m_ref.shape, NEG_BIG, F32)
    acc_ref[...] = jnp.zeros_like(acc_ref)
    extra = lax.broadcasted_iota(jnp.int32, (vt_ref.shape[0] - dh, seq), 0)
    vt_ref[dh:, :] = jnp.where(extra == 0, 1.0, 0.0).astype(BF16)

    def prologue(jb, carry):
        off = pl.multiple_of(jb * blk, blk)
        kb = k_ref[0, pl.ds(off, blk), :].astype(F32)
        kmean_ref[pl.ds(jb, 1), :] = jnp.sum(kb, axis=0, keepdims=True) * (1.0 / blk)
        vb = v_ref[0, pl.ds(off, blk), :].astype(F32)
        vt_ref[0:dh, pl.ds(off, blk)] = vb.T.astype(BF16)
        return carry

    lax.fori_loop(0, nb, prologue, 0, unroll=min(8, nb))

    km_parts = jnp.concatenate(_split2(kmean_ref[...]), axis=0)
    gate = _dot_nt(km_parts, q_ref[0])
    gate = gate[0:nb, :] + gate[nb:2 * nb, :]
    blk_ids = lax.broadcasted_iota(jnp.int32, (nb, seq), 0)
    own_blk = lax.shift_right_logical(lax.broadcasted_iota(jnp.int32, (nb, seq), 1),
                                      blk.bit_length() - 1)
    gate = jnp.where(blk_ids < own_blk, gate, -jnp.inf)
    sel = jnp.where(blk_ids == own_blk, 1.0, 0.0)
    blk_ids = blk_ids.astype(F32)
    for _ in range(MOBA_TOPK):
        mx = jnp.max(gate, axis=0, keepdims=True)
        first = jnp.min(jnp.where(gate == mx, blk_ids, float(nb)), axis=0, keepdims=True)
        hit = blk_ids == first
        sel = jnp.where(jnp.logical_and(hit, mx > -jnp.inf), 1.0, sel)
        gate = jnp.where(hit, -jnp.inf, gate)
    sel_ref[...] = sel

    key_minus_qry = (lax.broadcasted_iota(jnp.int32, (blk, blk), 0)
                     - lax.broadcasted_iota(jnp.int32, (blk, blk), 1))

    def score_tile(g, step, slot, nblocks=group, own_block=None):
        r0 = pl.multiple_of(g * gk, gk)
        i = jnp.minimum(step, nb - 1)
        qi = q_ref[0, pl.ds(pl.multiple_of(i * blk, blk), blk), :]
        cms = []
        for u in range(nblocks):
            rows = slice(u * blk, (u + 1) * blk)
            s = _dot_nt(k_ref[0, pl.ds(r0 + u * blk, blk), :], qi) + bias_ref[rows, :]
            if u == own_block:
                s = jnp.where(key_minus_qry <= 0, s, NEG_BIG)
            s_ref[slot, rows, :] = s
            cms.append(jnp.max(s, axis=0, keepdims=True))
        cms += [jnp.full((1, blk), NEG_BIG, F32)] * (group - nblocks)
        cmax_ref[slot] = jnp.concatenate(cms, axis=0)

    def softmax_tile(g, i, slot, nblocks=group):
        r0 = pl.multiple_of(g * gk, gk)
        off = slope_row * jnp.asarray(g * gk - i * blk, F32)
        chosen = sel_ref[pl.ds(pl.multiple_of(g * group, group), group),
                         pl.ds(pl.multiple_of(i * blk, blk), blk)] > 0.0
        m_old = m_ref[pl.ds(i, 1), :]
        m_new = jnp.maximum(m_old, jnp.max(jnp.where(chosen, cmax_ref[slot] + off, NEG_BIG),
                                           axis=0, keepdims=True))
        base = m_new - off
        for u in range(nblocks):
            shift = jnp.where(chosen[u:u + 1, :], base, POS_BIG)
            x = (s_ref[slot, u * blk:(u + 1) * blk, :] - shift).astype(BF16)
            p_ref[u * blk:(u + 1) * blk, :] = jnp.exp2(x)
        pv = _dot(vt_ref[:, pl.ds(r0, nblocks * blk)], p_ref[0:nblocks * blk, :])
        acc_ref[i] = jnp.exp2(m_old - m_new) * acc_ref[i] + pv
        m_ref[pl.ds(i, 1), :] = m_new

    def key_group(g, carry):
        first = g * group
        score_tile(g, first, 0, 1, 0)
        for r in range(group):
            if r + 1 < group:
                score_tile(g, first + r + 1, (r + 1) % 2, r + 2, r + 1)
            else:
                score_tile(g, first + group, group % 2)
            softmax_tile(g, first + r, r % 2, r + 1)

        @pl.when(first + group < nb)
        def _():
            score_tile(g, first + group + 1, 1)

            def tile_pair(i, score_ahead=True):
                for slot in range(2):
                    softmax_tile(g, i + slot, slot)
                    if score_ahead:
                        score_tile(g, i + 2 + slot, slot)

            def later_tiles(t, c):
                tile_pair(first + 2 * t)
                return c

            lax.fori_loop(group // 2, (nb - first) // 2 - 1, later_tiles, 0)
            tile_pair(nb - 2, score_ahead=False)

        return carry

    lax.fori_loop(0, nb // group, key_group, 0)

    def finalize(i, carry):
        a = acc_ref[i]
        out = a[0:dh, :] / a[dh:dh + 1, :]
        o_ref[0, pl.ds(pl.multiple_of(i * blk, blk), blk), :] = out.T.astype(o_ref.dtype)
        return carry

    lax.fori_loop(0, nb, finalize, 0, unroll=min(8, nb))


def _attention(q, k, v):
    b, s, w = q.shape
    dh = ATTN_HEAD_DIM
    nh = w // dh
    assert s % MOBA_BLOCK == 0 and s // MOBA_BLOCK >= MOBA_TOPK and dh == LANES
    nb = s // MOBA_BLOCK
    group = min(ATTN_GROUP, nb)
    assert nb % group == 0 and group % 2 == 0
    gk = group * MOBA_BLOCK
    slopes = 2.0 **(-8.0 * (jnp.arange(nh, dtype=F32) + 1.0) / nh)
    slopes = jnp.broadcast_to(slopes[:, None, None], (nh, SUBLANES, LANES))
    head_spec = pl.BlockSpec((1, s, dh), lambda bi, hi: (bi, 0, hi))
    return pl.pallas_call(
        _attn_kernel,
        out_shape=jax.ShapeDtypeStruct((b, s, w), BF16),
        grid=(b, nh),
        in_specs=[pl.BlockSpec((1, SUBLANES, LANES), lambda bi, hi: (hi, 0, 0)),
                  head_spec, head_spec, head_spec],
        out_specs=head_spec,
        scratch_shapes=[
            pltpu.VMEM((dh + BF16_ROWS, s), BF16),
            pltpu.VMEM((nb, dh), F32),
            pltpu.VMEM((nb, s), F32),
            pltpu.VMEM((gk, MOBA_BLOCK), F32),
            pltpu.VMEM((2, gk, MOBA_BLOCK), F32),
            pltpu.VMEM((2, group, MOBA_BLOCK), F32),
            pltpu.VMEM((gk, MOBA_BLOCK), BF16),
            pltpu.VMEM((nb, MOBA_BLOCK), F32),
            pltpu.VMEM((nb, dh + BF16_ROWS, MOBA_BLOCK), F32),
        ],
        compiler_params=pltpu.CompilerParams(
            dimension_semantics=("arbitrary", "arbitrary"), vmem_limit_bytes=VMEM_LIMIT),
        name="moba_attention",
    )(slopes, q, k, v)


def _ssm_kernel(xc_ref, dt_ref, zs_ref, dtb_ref, alog_ref, dskip_ref,
                normg_ref, expand_ref, o_ref, state_ref, y_ref):
    ts = xc_ref.shape[1]
    q = SSM_CHUNK
    hp = SSM_HEAD_DIM
    gw = SSM_WIDTH // SSM_GROUPS
    hpg = SSM_HEADS // SSM_GROUPS

    @pl.when(pl.program_id(1) == 0)
    def _():
        state_ref[...] = jnp.zeros_like(state_ref)

    a_neg = -jnp.exp(alog_ref[...])
    expand = expand_ref[...]
    row = lax.broadcasted_iota(jnp.int32, (q, q), 0)
    col = lax.broadcasted_iota(jnp.int32, (q, q), 1)
    lower = row >= col
    tri = jnp.where(lower, 1.0, 0.0).astype(BF16)
    first_head_lanes = lax.broadcasted_iota(jnp.int32, (q, 2 * hp), 1) < hp

    def expand_heads(w):
        hi, lo = _split2(w)
        return _dot(hi, expand) + _dot(lo, expand)

    def chunk(c, carry):
        r0 = pl.multiple_of(c * q, q)
        rows = pl.ds(r0, q)
        xs_b = xc_ref[0, rows, 0:SSM_WIDTH]
        xs = xs_b.astype(F32)
        dtr = dt_ref[0, rows, :] + dtb_ref[...]
        dt = jnp.maximum(dtr, 0.0) + jnp.log1p(jnp.exp(-jnp.abs(dtr)))
        da = dt * a_neg
        d_hi, d_mid, d_lo = _split3(da)
        a_cs = _dot(tri, d_hi) + _dot(tri, d_mid) + _dot(tri, d_lo)
        a_cs_t = a_cs.T
        dt_t = dt.T
        a_last = a_cs[q - 1:q, :]

        ea_x = expand_heads(jnp.exp(a_cs))
        wend_x = _dot((jnp.exp(a_last - a_cs) * dt).astype(BF16), expand)
        cdec_x = ea_x[q - 1:q, :]
        xs_w = (xs * wend_x).astype(BF16)

        for g in range(SSM_GROUPS):
            bg = xc_ref[0, rows, SSM_WIDTH + g * SSM_STATE:SSM_WIDTH + (g + 1) * SSM_STATE]
            cg = xc_ref[0, rows, SSM_WIDTH + SSM_BC + g * SSM_STATE:
                        SSM_WIDTH + SSM_BC + (g + 1) * SSM_STATE]
            cb = _dot_nt(cg, bg)
            lanes = slice(g * gw, (g + 1) * gw)
            st = state_ref[:, lanes]
            y_ref[:, lanes] = _dot(cg, st.astype(BF16)) * ea_x[:, lanes]
            state_ref[:, lanes] = st * cdec_x[:, lanes] + _dot_tn(bg, xs_w[:, lanes])
            for hh in range(0, hpg, 2):
                wts = []
                for h in (g * hpg + hh, g * hpg + hh + 1):
                    seg = a_cs[:, h:h + 1] - a_cs_t[h:h + 1, :]
                    w = cb * jnp.where(lower, jnp.exp(seg), 0.0) * dt_t[h:h + 1, :]
                    wts.append(w.astype(BF16))
                h0 = (g * hpg + hh) * hp
                x_pair = xs_b[:, h0:h0 + 2 * hp]
                x_diag = jnp.concatenate([jnp.where(first_head_lanes, x_pair, 0),
                                          jnp.where(first_head_lanes, 0, x_pair)], axis=0)
                y_ref[:, h0:h0 + 2 * hp] += _dot(jnp.concatenate(wts, axis=1), x_diag)

        y = (y_ref[...] + dskip_ref[...] * xs) * zs_ref[0, rows, :].astype(F32)
        o_ref[0, rows, :] = _rms(y, normg_ref[...]).astype(o_ref.dtype)
        return carry

    lax.fori_loop(0, ts // q, chunk, 0, unroll=4)


def _ssm(xc, dt_raw, zs, dt_bias, a_log, d_skip, norm_g):
    b, s, _ = xc.shape
    ts = min(SSM_TOKENS, s)
    assert s % ts == 0 and ts % SSM_CHUNK == 0 and SSM_HEADS <= LANES

    def lane_pad(x):
        return jnp.zeros((1, LANES), F32).at[0, :SSM_HEADS].set(x)

    head_of_lane = jnp.arange(SSM_WIDTH) // SSM_HEAD_DIM
    expand = (jnp.arange(LANES)[:, None] == head_of_lane[None, :]).astype(BF16)
    tile = lambda w: pl.BlockSpec((1, ts, w), lambda bi, ti: (bi, ti, 0))
    return pl.pallas_call(
        _ssm_kernel,
        out_shape=jax.ShapeDtypeStruct((b, s, SSM_WIDTH), BF16),
        grid=(b, s // ts),
        in_specs=[
            tile(SSM_CONV_CH), tile(LANES), tile(SSM_WIDTH),
            _resident((1, LANES)), _resident((1, LANES)),
            _resident((1, SSM_WIDTH)), _resident((1, SSM_WIDTH)),
            _resident((LANES, SSM_WIDTH)),
        ],
        out_specs=tile(SSM_WIDTH),
        scratch_shapes=[
            pltpu.VMEM((SSM_STATE, SSM_WIDTH), F32),
            pltpu.VMEM((SSM_CHUNK, SSM_WIDTH), F32),
        ],
        compiler_params=pltpu.CompilerParams(
            dimension_semantics=("arbitrary", "arbitrary"), vmem_limit_bytes=VMEM_LIMIT),
        name="ssd",
    )(xc, dt_raw, zs, lane_pad(dt_bias), lane_pad(a_log),
      jnp.repeat(d_skip, SSM_HEAD_DIM).reshape(1, -1), norm_g.reshape(1, -1), expand)


def _merge_kernel(h_ref, ya_ref, ys_ref, ua_ref, vn_ref, gs_ref, sw_ref,
                  sbt_ref, pa_ref, ps_ref, pc_ref, wo_ref, post_ref, o_ref, yc_ref):
    tm, d = h_ref.shape
    ch, gd = SGU_CHUNK, SGU_GROUP_DIM

    row = lax.broadcasted_iota(jnp.int32, (ch, ch), 0)
    col = lax.broadcasted_iota(jnp.int32, (ch, ch), 1)
    for g in range(SGU_GROUPS):
        w_sp = jnp.where(row >= col, sw_ref[g], 0.0).astype(BF16)
        bias = sbt_ref[:, g:g + 1]
        for c in range(tm // ch):
            rs, cs = slice(c * ch, (c + 1) * ch), slice(g * gd, (g + 1) * gd)
            sv = _dot(w_sp, vn_ref[rs, cs]) + bias
            yc_ref[rs, cs] = (ua_ref[rs, cs].astype(F32) * sv).astype(BF16)

    merged = gs_ref[:, 0:d].astype(F32) * _dot(ya_ref[...], pa_ref[...])
    merged += gs_ref[:, d:2 * d].astype(F32) * _dot(ys_ref[...], ps_ref[...])
    merged += gs_ref[:, 2 * d:3 * d].astype(F32) * _dot(yc_ref[...], pc_ref[...])
    m = _dot(merged.astype(BF16), wo_ref[...])
    o_ref[...] = h_ref[...] + _rms(m, post_ref[...])


def _merge(h, ya, ys, ua, vn, gs, sgu_w, sgu_b, p_attn, p_ssm, p_sgu, w_out, post_g, layer):
    t, d = h.shape
    tm = MERGE_TOKENS
    sw = SGU_GROUPS * SGU_GROUP_DIM
    assert t % tm == 0 and tm % SGU_CHUNK == 0
    tile = lambda w: pl.BlockSpec((tm, w), lambda i: (i, 0))
    return pl.pallas_call(
        _merge_kernel,
        out_shape=jax.ShapeDtypeStruct((t, d), F32),
        grid=(t // tm,),
        in_specs=[
            tile(d), tile(ya.shape[1]), tile(ys.shape[1]), tile(sw), tile(sw), tile(N_BRANCHES * d),
            _resident(sgu_w.shape), _resident((SGU_CHUNK, SGU_GROUPS)),
            _resident_layer(p_attn.shape, layer), _resident_layer(p_ssm.shape, layer),
            _resident_layer(p_sgu.shape, layer), _resident_layer(w_out.shape, layer),
            _resident((1, d)),
        ],
        out_specs=tile(d),
        scratch_shapes=[pltpu.VMEM((tm, sw), BF16)],
        compiler_params=pltpu.CompilerParams(
            dimension_semantics=("arbitrary",), vmem_limit_bytes=VMEM_LIMIT),
        name="merge",
    )(h, ya, ys, ua, vn, gs, sgu_w, sgu_b.T, p_attn, p_ssm, p_sgu, w_out, post_g.reshape(1, d))


def kernel(x, ffn1_pre_g, ffn1_w_gu, ffn1_w_down, ffn1_post_g, mix_pre_g, w_in, conv_w, conv_b, dt_bias, a_log, d_skip, ssm_norm_g, sgu_ln_g, sgu_ln_b, sgu_w, sgu_b, p_attn, p_ssm, p_sgu, w_out, mix_post_g, ffn2_pre_g, ffn2_w_gu, ffn2_w_down, ffn2_post_g):
    b, s, d = x.shape
    depth = w_in.shape[0]
    h = x.reshape(b * s, d)
    as_b = lambda a: a.astype(BF16)
    ffn1_w_gu, ffn1_w_down, ffn2_w_gu, ffn2_w_down = map(
        as_b, (ffn1_w_gu, ffn1_w_down, ffn2_w_gu, ffn2_w_down))
    p_attn, p_ssm, p_sgu, w_out = map(as_b, (p_attn, p_ssm, p_sgu, w_out))
    w_pieces = _split_w_in(w_in)
    ffn1 = lambda i: (ffn1_pre_g[i], ffn1_w_gu, ffn1_w_down, ffn1_post_g[i], i)
    ffn2 = lambda i: (ffn2_pre_g[i], ffn2_w_gu, ffn2_w_down, ffn2_post_g[i], i)
    h = _ffn(h, [ffn1(0)])
    for i in range(depth):
        q, k, v, zs, xc, dt_raw, ua, vn, gs = _inproj(
            h, mix_pre_g[i], w_pieces, conv_w[i], conv_b[i], sgu_ln_g[i], sgu_ln_b[i], s, i)
        seq = lambda a: a.reshape(b, s, a.shape[-1])
        ya = _attention(seq(q), seq(k), seq(v))
        ys = _ssm(seq(xc), seq(dt_raw), seq(zs), dt_bias[i], a_log[i], d_skip[i], ssm_norm_g[i])
        h = _merge(h, ya.reshape(b * s, -1), ys.reshape(b * s, -1), ua, vn, gs,
                   sgu_w[i], sgu_b[i], p_attn, p_ssm, p_sgu, w_out, mix_post_g[i], i)
        h = _ffn(h, [ffn2(i)] + ([ffn1(i + 1)] if i + 1 < depth else []))
    return h.reshape(b, s, d)
```

```python
import functools

import jax
import jax.numpy as jnp
from jax import lax
from jax.experimental import pallas as pl
from jax.experimental.pallas import tpu as pltpu

F32 = jnp.float32
BF16 = jnp.bfloat16

NORM_EPS = 1e-6
ATTN_HEADS = 8
ATTN_HEAD_DIM = 128
MOBA_BLOCK = 256
MOBA_TOPK = 3
SSM_HEADS = 16
SSM_HEAD_DIM = 64
SSM_WIDTH = SSM_HEADS * SSM_HEAD_DIM
SSM_GROUPS = 2
SSM_STATE = 128
SSM_CONV = 4
SSM_CHUNK = 128
SSM_BC = SSM_GROUPS * SSM_STATE
SSM_CONV_CH = SSM_WIDTH + 2 * SSM_BC
SGU_GROUPS = 8
SGU_GROUP_DIM = 128
SGU_CHUNK = 128
N_BRANCHES = 3

LANES = 128
SUBLANES = 8
BF16_ROWS = 16
LOG2E = 1.4426950408889634
VMEM_LIMIT = 56 * 1024 * 1024
NEG_BIG = -1e30
POS_BIG = 1e30

FFN_TOKENS = 512
FFN_HIDDEN_CHUNK = 256
INPROJ_TOKENS = 256
INPROJ_COL_CHUNK = 256
SSM_TOKENS = 512
MERGE_TOKENS = 512
ATTN_GROUP = 16


def _resident(shape):
    nd = len(shape)
    return pl.BlockSpec(shape, lambda *_: (0,) * nd, pipeline_mode=pl.Buffered(1))


def _resident_layer(stacked_shape, layer):
    shape = tuple(stacked_shape[1:])
    nd = len(shape)
    return pl.BlockSpec((None,) + shape, lambda *_: (layer,) + (0,) * nd,
                        pipeline_mode=pl.Buffered(1))


def _rms(x, g):
    return x * lax.rsqrt(jnp.mean(x * x, axis=-1, keepdims=True) + NORM_EPS) * g


def _sigmoid(x):
    return 1.0 / (1.0 + jnp.exp(-x))


def _gelu_tanh(x):
    return x * (0.5 * (1.0 + jnp.tanh(0.7978845608028654 * (x + 0.044715 * (x * x * x)))))


def _dot(a, b):
    return jnp.dot(a, b, preferred_element_type=F32)


def _dot_nt(a, b):
    return lax.dot_general(a, b, (((1,), (1,)), ((), ())), preferred_element_type=F32)


def _dot_tn(a, b):
    return lax.dot_general(a, b, (((0,), (0,)), ((), ())), preferred_element_type=F32)


def _split2(x):
    hi = x.astype(BF16)
    lo = (x - hi.astype(F32)).astype(BF16)
    return hi, lo


def _split3(x):
    hi = x.astype(BF16)
    r = x - hi.astype(F32)
    mid = r.astype(BF16)
    lo = (r - mid.astype(F32)).astype(BF16)
    return hi, mid, lo


def _ffn_kernel(h_ref, *refs):
    o_ref, acc_ref = refs[-2:]
    fc = FFN_HIDDEN_CHUNK
    x = h_ref[...]
    for k in range((len(refs) - 2) // 4):
        pre_ref, wgu_ref, wd_ref, post_ref = refs[4 * k:4 * k + 4]
        xn = _rms(x, pre_ref[...]).astype(BF16)
        hidden = wd_ref.shape[0]
        for c in range(hidden // fc):
            g = _dot(xn, wgu_ref[:, c * fc:(c + 1) * fc])
            u = _dot(xn, wgu_ref[:, hidden + c * fc:hidden + (c + 1) * fc])
            a = (g * _sigmoid(g) * u).astype(BF16)
            d = _dot(a, wd_ref[c * fc:(c + 1) * fc, :])
            if c == 0:
                acc_ref[...] = d
            else:
                acc_ref[...] += d
        x = x + 0.5 * _rms(acc_ref[...], post_ref[...])
    o_ref[...] = x


def _ffn(h, stages):
    t, d = h.shape
    tm = FFN_TOKENS
    assert t % tm == 0
    in_specs, args = [pl.BlockSpec((tm, d), lambda i: (i, 0))], [h]
    for pre_g, w_gu, w_down, post_g, layer in stages:
        assert w_down.shape[1] % FFN_HIDDEN_CHUNK == 0
        in_specs += [_resident((1, d)), _resident_layer(w_gu.shape, layer),
                     _resident_layer(w_down.shape, layer), _resident((1, d))]
        args += [pre_g.reshape(1, d), w_gu, w_down, post_g.reshape(1, d)]
    return pl.pallas_call(
        _ffn_kernel,
        out_shape=jax.ShapeDtypeStruct((t, d), F32),
        grid=(t // tm,),
        in_specs=in_specs,
        out_specs=pl.BlockSpec((tm, d), lambda i: (i, 0)),
        scratch_shapes=[pltpu.VMEM((tm, d), F32)],
        compiler_params=pltpu.CompilerParams(
            dimension_semantics=("arbitrary",), vmem_limit_bytes=VMEM_LIMIT),
        name="ffn",
    )(*args)


def _inproj_layout(d_model):
    aw = ATTN_HEADS * ATTN_HEAD_DIM
    sw = SGU_GROUPS * SGU_GROUP_DIM
    return (("q", aw, aw), ("k", aw, aw), ("v", aw, aw), ("z", SSM_WIDTH, SSM_WIDTH),
            ("xbc", SSM_CONV_CH, SSM_CONV_CH), ("dt", SSM_HEADS, LANES),
            ("u", sw, sw), ("vg", sw, sw), ("gates", N_BRANCHES * d_model, N_BRANCHES * d_model))


def _inproj_kernel(*refs, tiles_per_seq, n_pieces):
    h_ref, pre_ref = refs[:2]
    w_refs = refs[2:2 + n_pieces]
    (convw_ref, convb_ref, lng_ref, lnb_ref,
     q_ref, k_ref, v_ref, zs_ref, xc_ref, dt_ref, ua_ref, vn_ref, gs_ref,
     xraw_ref, va_ref, xn_ref) = refs[2 + n_pieces:]
    tm = h_ref.shape[0]
    pad = SUBLANES

    @pl.when(pl.program_id(0) % tiles_per_seq == 0)
    def _():
        xraw_ref[0:pad, :] = jnp.zeros((pad, SSM_CONV_CH), F32)

    xn_ref[...] = _rms(h_ref[...], pre_ref[...]).astype(BF16)

    def store_q(cols, r):
        q_ref[:, cols] = (r * (ATTN_HEAD_DIM ** -0.5 * LOG2E)).astype(BF16)

    def store_plain(o_ref):
        def store(cols, r):
            o_ref[:, cols] = r.astype(o_ref.dtype)
        return store

    def store_silu(cols, r):
        zs_ref[:, cols] = (r * _sigmoid(r)).astype(BF16)

    def store_conv(cols, r):
        xraw_ref[pad:pad + tm, cols] = r
        conv = convb_ref[:, cols] + convw_ref[SSM_CONV - 1:SSM_CONV, cols] * r
        for kk in range(SSM_CONV - 1):
            lag = SSM_CONV - 1 - kk
            conv = conv + convw_ref[kk:kk + 1, cols] * xraw_ref[pad - lag:pad - lag + tm, cols]
        xc_ref[:, cols] = (conv * _sigmoid(conv)).astype(BF16)

    def store_gelu(cols, r):
        ua_ref[:, cols] = _gelu_tanh(r).astype(BF16)

    def store_gelu_f32(cols, r):
        va_ref[:, cols] = _gelu_tanh(r)

    def store_sigmoid(cols, r):
        gs_ref[:, cols] = _sigmoid(r).astype(BF16)

    def layer_norm_v():
        va = va_ref[...]
        mu = jnp.mean(va, axis=-1, keepdims=True)
        var = jnp.mean(jnp.square(va - mu), axis=-1, keepdims=True)
        vn = (va - mu) * lax.rsqrt(var + NORM_EPS) * lng_ref[...] + lnb_ref[...]
        vn_ref[...] = vn.astype(BF16)

    def carry_rows():
        xraw_ref[0:pad, :] = xraw_ref[tm:tm + pad, :]

    stores = dict(q=store_q, k=store_plain(k_ref), v=store_plain(v_ref), z=store_silu,
                  xbc=store_conv, dt=store_plain(dt_ref), u=store_gelu, vg=store_gelu_f32,
                  gates=store_sigmoid)
    after = dict(vg=layer_norm_v, xbc=carry_rows)
    offsets, piece, off = {}, 0, 0
    for name, _, width in _inproj_layout(h_ref.shape[1]):
        if off == w_refs[piece].shape[1]:
            piece, off = piece + 1, 0
        offsets[name] = (w_refs[piece], off, width)
        off += width

    def chunks(names):
        out = []
        for name in names:
            w_ref, off, width = offsets[name]
            starts = list(range(0, width, INPROJ_COL_CHUNK))
            for c0 in starts:
                n = min(INPROJ_COL_CHUNK, width - c0)
                out.append((name, w_ref, off + c0, c0, n, c0 == starts[-1]))
        return out

    heavy = chunks(("vg", "xbc", "u", "z"))
    light = chunks(("q", "k", "v", "dt", "gates"))
    per_heavy = len(light) // len(heavy)
    order = []
    while heavy or light:
        order += heavy[:1] + light[:per_heavy]
        heavy, light = heavy[1:], light[per_heavy:]
    for name, w_ref, w0, c0, n, last in order:
        stores[name](slice(c0, c0 + n), _dot(xn_ref[...], w_ref[:, w0:w0 + n]))
        if last and name in after:
            after[name]()


def _inproj(h, pre_g, w_pieces, conv_w, conv_b, ln_g, ln_b, seq_len, layer):
    t, d = h.shape
    tm = INPROJ_TOKENS
    layout = _inproj_layout(d)
    sw = SGU_GROUPS * SGU_GROUP_DIM
    assert seq_len % tm == 0 and t % seq_len == 0
    assert sum(w.shape[2] for w in w_pieces) == sum(p for _, _, p in layout)
    out_shape = tuple(jax.ShapeDtypeStruct((t, p), F32 if name == "dt" else BF16)
                      for name, _, p in layout)
    out_specs = tuple(pl.BlockSpec((tm, p), lambda i: (i, 0)) for _, _, p in layout)
    return pl.pallas_call(
        functools.partial(_inproj_kernel, tiles_per_seq=seq_len // tm, n_pieces=len(w_pieces)),
        out_shape=out_shape,
        grid=(t // tm,),
        in_specs=[
            pl.BlockSpec((tm, d), lambda i: (i, 0)),
            _resident((1, d)),
            *[_resident_layer(w.shape, layer) for w in w_pieces],
            _resident((SSM_CONV, SSM_CONV_CH)), _resident((1, SSM_CONV_CH)),
            _resident((1, sw)), _resident((1, sw)),
        ],
        out_specs=out_specs,
        scratch_shapes=[
            pltpu.VMEM((tm + SUBLANES, SSM_CONV_CH), F32),
            pltpu.VMEM((tm, sw), F32),
            pltpu.VMEM((tm, d), BF16),
        ],
        compiler_params=pltpu.CompilerParams(
            dimension_semantics=("arbitrary",), vmem_limit_bytes=VMEM_LIMIT),
        name="inproj",
    )(h, pre_g.reshape(1, d), *w_pieces, conv_w, conv_b.reshape(1, -1), ln_g.reshape(1, -1),
      ln_b.reshape(1, -1))


def _inproj_pieces(d_model):
    runs, start, off = [], 0, 0
    for _, width, padded in _inproj_layout(d_model):
        if padded > width:
            if off > start:
                runs.append((start, off, off - start))
            runs.append((off, off + width, padded))
            start = off + width
        off += width
    if off > start:
        runs.append((start, off, off - start))
    return runs


def _split_w_in(w_in):
    depth, d, cols = w_in.shape
    out = []
    for start, stop, padded in _inproj_pieces(d):
        piece = w_in[:, :, start:stop].astype(BF16)
        if padded > stop - start:
            piece = jnp.pad(piece, ((0, 0), (0, 0), (0, padded - (stop - start))))
        out.append(piece)
    assert stop == cols
    return tuple(out)


def _attn_kernel(slope_ref, q_ref, k_ref, v_ref, o_ref, vt_ref, kmean_ref, sel_ref, bias_ref,
                 s_ref, cmax_ref, p_ref, m_ref, acc_ref):
    blk = MOBA_BLOCK
    dh = ATTN_HEAD_DIM
    seq = q_ref.shape[1]
    nb = seq // blk
    gk = bias_ref.shape[0]
    group = gk // blk
    slope = slope_ref[0, 0:1, :] * LOG2E
    slope_row = jnp.concatenate([slope] * (blk // LANES), axis=1)
    bias_ref[...] = lax.broadcasted_iota(jnp.int32, (gk, blk), 0).astype(F32) * slope_row
    m_ref[...] = jnp.full(m_ref.shape, NEG_BIG, F32)
    acc_ref[...] = jnp.zeros_like(acc_ref)
    extra = lax.broadcasted_iota(jnp.int32, (vt_ref.shape[0] - dh, seq), 0)
    vt_ref[dh:, :] = jnp.where(extra == 0, 1.0, 0.0).astype(BF16)

    def prologue(jb, carry):
        off = pl.multiple_of(jb * blk, blk)
        kb = k_ref[0, pl.ds(off, blk), :].astype(F32)
        kmean_ref[pl.ds(jb, 1), :] = jnp.sum(kb, axis=0, keepdims=True) * (1.0 / blk)
        vb = v_ref[0, pl.ds(off, blk), :].astype(F32)
        vt_ref[0:dh, pl.ds(off, blk)] = vb.T.astype(BF16)
        return carry

    lax.fori_loop(0, nb, prologue, 0, unroll=min(8, nb))

    km_parts = jnp.concatenate(_split2(kmean_ref[...]), axis=0)
    gate = _dot_nt(km_parts, q_ref[0])
    gate = gate[0:nb, :] + gate[nb:2 * nb, :]
    blk_ids = lax.broadcasted_iota(jnp.int32, (nb, seq), 0)
    own_blk = lax.shift_right_logical(lax.broadcasted_iota(jnp.int32, (nb, seq), 1),
                                      blk.bit_length() - 1)
    gate = jnp.where(blk_ids < own_blk, gate, -jnp.inf)
    sel = jnp.where(blk_ids == own_blk, 1.0, 0.0)
    blk_ids = blk_ids.astype(F32)
    for _ in range(MOBA_TOPK):
        mx = jnp.max(gate, axis=0, keepdims=True)
        first = jnp.min(jnp.where(gate == mx, blk_ids, float(nb)), axis=0, keepdims=True)
        hit = blk_ids == first
        sel = jnp.where(jnp.logical_and(hit, mx > -jnp.inf), 1.0, sel)
        gate = jnp.where(hit, -jnp.inf, gate)
    sel_ref[...] = sel

    key_minus_qry = (lax.broadcasted_iota(jnp.int32, (blk, blk), 0)
                     - lax.broadcasted_iota(jnp.int32, (blk, blk), 1))

    def score_tile(g, step, slot, nblocks=group, own_block=None):
        r0 = pl.multiple_of(g * gk, gk)
        i = jnp.minimum(step, nb - 1)
        qi = q_ref[0, pl.ds(pl.multiple_of(i * blk, blk), blk), :]
        cms = []
        for u in range(nblocks):
            rows = slice(u * blk, (u + 1) * blk)
            s = _dot_nt(k_ref[0, pl.ds(r0 + u * blk, blk), :], qi) + bias_ref[rows, :]
            if u == own_block:
                s = jnp.where(key_minus_qry <= 0, s, NEG_BIG)
            s_ref[slot, rows, :] = s
            cms.append(jnp.max(s, axis=0, keepdims=True))
        cms += [jnp.full((1, blk), NEG_BIG, F32)] * (group - nblocks)
        cmax_ref[slot] = jnp.concatenate(cms, axis=0)

    def softmax_tile(g, i, slot, nblocks=group):
        r0 = pl.multiple_of(g * gk, gk)
        off = slope_row * jnp.asarray(g * gk - i * blk, F32)
        chosen = sel_ref[pl.ds(pl.multiple_of(g * group, group), group),
                         pl.ds(pl.multiple_of(i * blk, blk), blk)] > 0.0
        m_old = m_ref[pl.ds(i, 1), :]
        m_new = jnp.maximum(m_old, jnp.max(jnp.where(chosen, cmax_ref[slot] + off, NEG_BIG),
                                           axis=0, keepdims=True))
        base = m_new - off
        for u in range(nblocks):
            shift = jnp.where(chosen[u:u + 1, :], base, POS_BIG)
            x = (s_ref[slot, u * blk:(u + 1) * blk, :] - shift).astype(BF16)
            p_ref[u * blk:(u + 1) * blk, :] = jnp.exp2(x)
        pv = _dot(vt_ref[:, pl.ds(r0, nblocks * blk)], p_ref[0:nblocks * blk, :])
        acc_ref[i] = jnp.exp2(m_old - m_new) * acc_ref[i] + pv
        m_ref[pl.ds(i, 1), :] = m_new

    def key_group(g, carry):
        first = g * group
        score_tile(g, first, 0, 1, 0)
        for r in range(group):
            if r + 1 < group:
                score_tile(g, first + r + 1, (r + 1) % 2, r + 2, r + 1)
            else:
                score_tile(g, first + group, group % 2)
            softmax_tile(g, first + r, r % 2, r + 1)

        @pl.when(first + group < nb)
        def _():
            score_tile(g, first + group + 1, 1)

            def tile_pair(i, score_ahead=True):
                for slot in range(2):
                    softmax_tile(g, i + slot, slot)
                    if score_ahead:
                        score_tile(g, i + 2 + slot, slot)

            def later_tiles(t, c):
                tile_pair(first + 2 * t)
                return c

            lax.fori_loop(group // 2, (nb - first) // 2 - 1, later_tiles, 0)
            tile_pair(nb - 2, score_ahead=False)

        return carry

    lax.fori_loop(0, nb // group, key_group, 0)

    def finalize(i, carry):
        a = acc_ref[i]
        out = a[0:dh, :] / a[dh:dh + 1, :]
        o_ref[0, pl.ds(pl.multiple_of(i * blk, blk), blk), :] = out.T.astype(o_ref.dtype)
        return carry

    lax.fori_loop(0, nb, finalize, 0, unroll=min(8, nb))


def _attention(q, k, v):
    b, s, w = q.shape
    dh = ATTN_HEAD_DIM
    nh = w // dh
    assert s % MOBA_BLOCK == 0 and s // MOBA_BLOCK >= MOBA_TOPK and dh == LANES
    nb = s // MOBA_BLOCK
    group = min(ATTN_GROUP, nb)
    assert nb % group == 0 and group % 2 == 0
    gk = group * MOBA_BLOCK
    slopes = 2.0 **(-8.0 * (jnp.arange(nh, dtype=F32) + 1.0) / nh)
    slopes = jnp.broadcast_to(slopes[:, None, None], (nh, SUBLANES, LANES))
    head_spec = pl.BlockSpec((1, s, dh), lambda bi, hi: (bi, 0, hi))
    return pl.pallas_call(
        _attn_kernel,
        out_shape=jax.ShapeDtypeStruct((b, s, w), BF16),
        grid=(b, nh),
        in_specs=[pl.BlockSpec((1, SUBLANES, LANES), lambda bi, hi: (hi, 0, 0)),
                  head_spec, head_spec, head_spec],
        out_specs=head_spec,
        scratch_shapes=[
            pltpu.VMEM((dh + BF16_ROWS, s), BF16),
            pltpu.VMEM((nb, dh), F32),
            pltpu.VMEM((nb, s), F32),
            pltpu.VMEM((gk, MOBA_BLOCK), F32),
            pltpu.VMEM((2, gk, MOBA_BLOCK), F32),
            pltpu.VMEM((2, group, MOBA_BLOCK), F32),
            pltpu.VMEM((gk, MOBA_BLOCK), BF16),
            pltpu.VMEM((nb, MOBA_BLOCK), F32),
            pltpu.VMEM((nb, dh + BF16_ROWS, MOBA_BLOCK), F32),
        ],
        compiler_params=pltpu.CompilerParams(
            dimension_semantics=("arbitrary", "arbitrary"), vmem_limit_bytes=VMEM_LIMIT),
        name="moba_attention",
    )(slopes, q, k, v)


def _ssm_kernel(xc_ref, dt_ref, zs_ref, dtb_ref, alog_ref, dskip_ref,
                normg_ref, expand_ref, o_ref, state_ref, y_ref):
    ts = xc_ref.shape[1]
    q = SSM_CHUNK
    hp = SSM_HEAD_DIM
    gw = SSM_WIDTH // SSM_GROUPS
    hpg = SSM_HEADS // SSM_GROUPS

    @pl.when(pl.program_id(1) == 0)
    def _():
        state_ref[...] = jnp.zeros_like(state_ref)

    a_neg = -jnp.exp(alog_ref[...])
    expand = expand_ref[...]
    row = lax.broadcasted_iota(jnp.int32, (q, q), 0)
    col = lax.broadcasted_iota(jnp.int32, (q, q), 1)
    lower = row >= col
    tri = jnp.where(lower, 1.0, 0.0).astype(BF16)
    first_head_lanes = lax.broadcasted_iota(jnp.int32, (q, 2 * hp), 1) < hp

    def expand_heads(w):
        hi, lo = _split2(w)
        return _dot(hi, expand) + _dot(lo, expand)

    def chunk(c, carry):
        r0 = pl.multiple_of(c * q, q)
        rows = pl.ds(r0, q)
        xs_b = xc_ref[0, rows, 0:SSM_WIDTH]
        xs = xs_b.astype(F32)
        dtr = dt_ref[0, rows, :] + dtb_ref[...]
        dt = jnp.maximum(dtr, 0.0) + jnp.log1p(jnp.exp(-jnp.abs(dtr)))
        da = dt * a_neg
        d_hi, d_mid, d_lo = _split3(da)
        a_cs = _dot(tri, d_hi) + _dot(tri, d_mid) + _dot(tri, d_lo)
        a_cs_t = a_cs.T
        dt_t = dt.T
        a_last = a_cs[q - 1:q, :]

        ea_x = expand_heads(jnp.exp(a_cs))
        wend_x = _dot((jnp.exp(a_last - a_cs) * dt).astype(BF16), expand)
        cdec_x = ea_x[q - 1:q, :]
        xs_w = (xs * wend_x).astype(BF16)

        for g in range(SSM_GROUPS):
            bg = xc_ref[0, rows, SSM_WIDTH + g * SSM_STATE:SSM_WIDTH + (g + 1) * SSM_STATE]
            cg = xc_ref[0, rows, SSM_WIDTH + SSM_BC + g * SSM_STATE:
                        SSM_WIDTH + SSM_BC + (g + 1) * SSM_STATE]
            cb = _dot_nt(cg, bg)
            lanes = slice(g * gw, (g + 1) * gw)
            st = state_ref[:, lanes]
            y_ref[:, lanes] = _dot(cg, st.astype(BF16)) * ea_x[:, lanes]
            state_ref[:, lanes] = st * cdec_x[:, lanes] + _dot_tn(bg, xs_w[:, lanes])
            for hh in range(0, hpg, 2):
                wts = []
                for h in (g * hpg + hh, g * hpg + hh + 1):
                    seg = a_cs[:, h:h + 1] - a_cs_t[h:h + 1, :]
                    w = cb * jnp.where(lower, jnp.exp(seg), 0.0) * dt_t[h:h + 1, :]
                    wts.append(w.astype(BF16))
                h0 = (g * hpg + hh) * hp
                x_pair = xs_b[:, h0:h0 + 2 * hp]
                x_diag = jnp.concatenate([jnp.where(first_head_lanes, x_pair, 0),
                                          jnp.where(first_head_lanes, 0, x_pair)], axis=0)
                y_ref[:, h0:h0 + 2 * hp] += _dot(jnp.concatenate(wts, axis=1), x_diag)

        y = (y_ref[...] + dskip_ref[...] * xs) * zs_ref[0, rows, :].astype(F32)
        o_ref[0, rows, :] = _rms(y, normg_ref[...]).astype(o_ref.dtype)
        return carry

    lax.fori_loop(0, ts // q, chunk, 0, unroll=4)


def _ssm(xc, dt_raw, zs, dt_bias, a_log, d_skip, norm_g):
    b, s, _ = xc.shape
    ts = min(SSM_TOKENS, s)
    assert s % ts == 0 and ts % SSM_CHUNK == 0 and SSM_HEADS <= LANES

    def lane_pad(x):
        return jnp.zeros((1, LANES), F32).at[0, :SSM_HEADS].set(x)

    head_of_lane = jnp.arange(SSM_WIDTH) // SSM_HEAD_DIM
    expand = (jnp.arange(LANES)[:, None] == head_of_lane[None, :]).astype(BF16)
    tile = lambda w: pl.BlockSpec((1, ts, w), lambda bi, ti: (bi, ti, 0))
    return pl.pallas_call(
        _ssm_kernel,
        out_shape=jax.ShapeDtypeStruct((b, s, SSM_WIDTH), BF16),
        grid=(b, s // ts),
        in_specs=[
            tile(SSM_CONV_CH), tile(LANES), tile(SSM_WIDTH),
            _resident((1, LANES)), _resident((1, LANES)),
            _resident((1, SSM_WIDTH)), _resident((1, SSM_WIDTH)),
            _resident((LANES, SSM_WIDTH)),
        ],
        out_specs=tile(SSM_WIDTH),
        scratch_shapes=[
            pltpu.VMEM((SSM_STATE, SSM_WIDTH), F32),
            pltpu.VMEM((SSM_CHUNK, SSM_WIDTH), F32),
        ],
        compiler_params=pltpu.CompilerParams(
            dimension_semantics=("arbitrary", "arbitrary"), vmem_limit_bytes=VMEM_LIMIT),
        name="ssd",
    )(xc, dt_raw, zs, lane_pad(dt_bias), lane_pad(a_log),
      jnp.repeat(d_skip, SSM_HEAD_DIM).reshape(1, -1), norm_g.reshape(1, -1), expand)


def _merge_kernel(h_ref, ya_ref, ys_ref, ua_ref, vn_ref, gs_ref, sw_ref,
                  sbt_ref, pa_ref, ps_ref, pc_ref, wo_ref, post_ref, o_ref, yc_ref):
    tm, d = h_ref.shape
    ch, gd = SGU_CHUNK, SGU_GROUP_DIM

    row = lax.broadcasted_iota(jnp.int32, (ch, ch), 0)
    col = lax.broadcasted_iota(jnp.int32, (ch, ch), 1)
    for g in range(SGU_GROUPS):
        w_sp = jnp.where(row >= col, sw_ref[g], 0.0).astype(BF16)
        bias = sbt_ref[:, g:g + 1]
        for c in range(tm // ch):
            rs, cs = slice(c * ch, (c + 1) * ch), slice(g * gd, (g + 1) * gd)
            sv = _dot(w_sp, vn_ref[rs, cs]) + bias
            yc_ref[rs, cs] = (ua_ref[rs, cs].astype(F32) * sv).astype(BF16)

    merged = gs_ref[:, 0:d].astype(F32) * _dot(ya_ref[...], pa_ref[...])
    merged += gs_ref[:, d:2 * d].astype(F32) * _dot(ys_ref[...], ps_ref[...])
    merged += gs_ref[:, 2 * d:3 * d].astype(F32) * _dot(yc_ref[...], pc_ref[...])
    m = _dot(merged.astype(BF16), wo_ref[...])
    o_ref[...] = h_ref[...] + _rms(m, post_ref[...])


def _merge(h, ya, ys, ua, vn, gs, sgu_w, sgu_b, p_attn, p_ssm, p_sgu, w_out, post_g, layer):
    t, d = h.shape
    tm = MERGE_TOKENS
    sw = SGU_GROUPS * SGU_GROUP_DIM
    assert t % tm == 0 and tm % SGU_CHUNK == 0
    tile = lambda w: pl.BlockSpec((tm, w), lambda i: (i, 0))
    return pl.pallas_call(
        _merge_kernel,
        out_shape=jax.ShapeDtypeStruct((t, d), F32),
        grid=(t // tm,),
        in_specs=[
            tile(d), tile(ya.shape[1]), tile(ys.shape[1]), tile(sw), tile(sw), tile(N_BRANCHES * d),
            _resident(sgu_w.shape), _resident((SGU_CHUNK, SGU_GROUPS)),
            _resident_layer(p_attn.shape, layer), _resident_layer(p_ssm.shape, layer),
            _resident_layer(p_sgu.shape, layer), _resident_layer(w_out.shape, layer),
            _resident((1, d)),
        ],
        out_specs=tile(d),
        scratch_shapes=[pltpu.VMEM((tm, sw), BF16)],
        compiler_params=pltpu.CompilerParams(
            dimension_semantics=("arbitrary",), vmem_limit_bytes=VMEM_LIMIT),
        name="merge",
    )(h, ya, ys, ua, vn, gs, sgu_w, sgu_b.T, p_attn, p_ssm, p_sgu, w_out, post_g.reshape(1, d))


def kernel(x, ffn1_pre_g, ffn1_w_gu, ffn1_w_down, ffn1_post_g, mix_pre_g, w_in, conv_w, conv_b, dt_bias, a_log, d_skip, ssm_norm_g, sgu_ln_g, sgu_ln_b, sgu_w, sgu_b, p_attn, p_ssm, p_sgu, w_out, mix_post_g, ffn2_pre_g, ffn2_w_gu, ffn2_w_down, ffn2_post_g):
    b, s, d = x.shape
    depth = w_in.shape[0]
    h = x.reshape(b * s, d)
    as_b = lambda a: a.astype(BF16)
    ffn1_w_gu, ffn1_w_down, ffn2_w_gu, ffn2_w_down = map(
        as_b, (ffn1_w_gu, ffn1_w_down, ffn2_w_gu, ffn2_w_down))
    p_attn, p_ssm, p_sgu, w_out = map(as_b, (p_attn, p_ssm, p_sgu, w_out))
    w_pieces = _split_w_in(w_in)
    ffn1 = lambda i: (ffn1_pre_g[i], ffn1_w_gu, ffn1_w_down, ffn1_post_g[i], i)
    ffn2 = lambda i: (ffn2_pre_g[i], ffn2_w_gu, ffn2_w_down, ffn2_post_g[i], i)
    h = _ffn(h, [ffn1(0)])
    for i in range(depth):
        q, k, v, zs, xc, dt_raw, ua, vn, gs = _inproj(
            h, mix_pre_g[i], w_pieces, conv_w[i], conv_b[i], sgu_ln_g[i], sgu_ln_b[i], s, i)
        seq = lambda a: a.reshape(b, s, a.shape[-1])
        ya = _attention(seq(q), seq(k), seq(v))
        ys = _ssm(seq(xc), seq(dt_raw), seq(zs), dt_bias[i], a_log[i], d_skip[i], ssm_norm_g[i])
        h = _merge(h, ya.reshape(b * s, -1), ys.reshape(b * s, -1), ua, vn, gs,
                   sgu_w[i], sgu_b[i], p_attn, p_ssm, p_sgu, w_out, mix_post_g[i], i)
        h = _ffn(h, [ffn2(i)] + ([ffn1(i + 1)] if i + 1 < depth else []))
    return h.reshape(b, s, d)
```
